```python
import jax, jax.numpy as jnp
from jax import lax
import numpy as np

D_MODEL = 2048
BATCH = 2
SEQ = 8192
DEPTH = 4

N_BRANCH = 4
BRANCH_W = D_MODEL // 4
FOX_HEADS = 8
FOX_HD = BRANCH_W // FOX_HEADS
Q_BLOCK = 128
CONV_W = BRANCH_W
CONV_K = 31
GLA_HEADS = 4
GLA_DK = BRANCH_W // 2
GLA_DV = BRANCH_W
GLA_HK = GLA_DK // GLA_HEADS
GLA_HV = GLA_DV // GLA_HEADS
GLA_RANK = 16
GLA_TEMP = 16.0
GLA_CHUNK = 16
POOL_GROUPS = 4
POOL_GW = BRANCH_W // POOL_GROUPS
POOL_WINDOWS = (2, 4, 8, 16)
D_FF = 5632
FFN_K = 3
EPS = 1e-6

IN_SIZES = (BRANCH_W, BRANCH_W, BRANCH_W, FOX_HEADS,
            2 * CONV_W,
            GLA_DK, GLA_DK, GLA_DV, GLA_RANK, GLA_DV,
            BRANCH_W,
            N_BRANCH * D_MODEL)
D_IN = sum(IN_SIZES)

kernel_name = "hybrid_fox_conformer_gla_pool_block"


def rmsnorm(x, g):
    xf = x.astype(jnp.float32)
    y = xf * lax.rsqrt(jnp.mean(xf * xf, axis=-1, keepdims=True) + EPS)
    return (y * g.astype(jnp.float32)).astype(x.dtype)


def layernorm(x, g, b):
    xf = x.astype(jnp.float32)
    mu = jnp.mean(xf, axis=-1, keepdims=True)
    var = jnp.mean(jnp.square(xf - mu), axis=-1, keepdims=True)
    y = (xf - mu) * lax.rsqrt(var + EPS)
    return (y * g.astype(jnp.float32) + b.astype(jnp.float32)).astype(x.dtype)


def causal_dwconv(x, w, b):
    K, C = w.shape
    y = lax.conv_general_dilated(x, w[:, None, :].astype(x.dtype), window_strides=(1,),
                                 padding=[(K - 1, 0)], dimension_numbers=('NWC', 'WIO', 'NWC'),
                                 feature_group_count=C)
    return y + b.astype(x.dtype)


def fox_mixer(q, k, v, f_logit, f_b, q_g, k_g):
    B, S, _ = q.shape
    dt = q.dtype
    def heads(t):
        return t.reshape(B, S, FOX_HEADS, FOX_HD).transpose(0, 2, 1, 3)
    qh = rmsnorm(heads(q), q_g)
    kh = rmsnorm(heads(k), k_g)
    vh = heads(v)
    logf = jax.nn.log_sigmoid((f_logit + f_b).astype(jnp.float32))
    c = jnp.cumsum(logf, axis=1).transpose(0, 2, 1)
    scale = FOX_HD ** -0.5
    outs = []
    for i in range(S // Q_BLOCK):
        lo, hi = i * Q_BLOCK, (i + 1) * Q_BLOCK
        s = jnp.einsum('bhtd,bhsd->bhts', qh[:, :, lo:hi], kh[:, :, :hi]).astype(jnp.float32) * scale
        s = s + c[:, :, lo:hi, None] - c[:, :, None, :hi]
        mask = jnp.arange(hi)[None, :] <= jnp.arange(lo, hi)[:, None]
        p = jax.nn.softmax(jnp.where(mask, s, -jnp.inf), axis=-1).astype(dt)
        outs.append(jnp.einsum('bhts,bhsd->bhtd', p, vh[:, :, :hi]))
    o = jnp.concatenate(outs, axis=2)
    return o.transpose(0, 2, 1, 3).reshape(B, S, BRANCH_W)


def conformer_conv(z, dw, db, ln_g, ln_b):
    a, gate = jnp.split(z, 2, axis=-1)
    u = a * jax.nn.sigmoid(gate)
    u = causal_dwconv(u, dw, db)
    u = layernorm(u, ln_g, ln_b)
    return jax.nn.silu(u)


def gla_mixer(q, k, v, a_low, r, wa, ba, og):
    B, S, _ = q.shape
    dt = q.dtype
    C = GLA_CHUNK
    N = S // C
    loga = jax.nn.log_sigmoid((a_low @ wa + ba).astype(jnp.float32)) / GLA_TEMP
    def chunks(t, d):
        return t.astype(jnp.float32).reshape(B, N, C, GLA_HEADS, d).transpose(0, 3, 1, 2, 4)
    qc = chunks(q, GLA_HK) * (GLA_HK ** -0.5)
    kc = chunks(k, GLA_HK)
    vc = chunks(v, GLA_HV)
    bc = jnp.cumsum(chunks(loga, GLA_HK), axis=3)
    b_last = bc[:, :, :, -1:, :]
    diff = bc[:, :, :, :, None, :] - bc[:, :, :, None, :, :]
    mask = jnp.tril(jnp.ones((C, C), dtype=bool))[:, :, None]
    attn = jnp.sum(qc[:, :, :, :, None, :] * kc[:, :, :, None, :, :]
                   * jnp.exp(jnp.where(mask, diff, -jnp.inf)), axis=-1)
    o_intra = jnp.einsum('bhnts,bhnsv->bhntv', attn, vc)
    q_dec = qc * jnp.exp(bc)
    k_dec = kc * jnp.exp(b_last - bc)
    decay = jnp.exp(b_last[:, :, :, 0, :])
    def step(state, xs):
        qd, kd, vv, dec = xs
        o = jnp.einsum('bhcd,bhdv->bhcv', qd, state)
        state = dec[..., None] * state + jnp.einsum('bhcd,bhcv->bhdv', kd, vv)
        return state, o
    xs = (jnp.moveaxis(q_dec, 2, 0), jnp.moveaxis(k_dec, 2, 0), jnp.moveaxis(vc, 2, 0), jnp.moveaxis(decay, 2, 0))
    state0 = jnp.zeros((B, GLA_HEADS, GLA_HK, GLA_HV), jnp.float32)
    _, o_inter = lax.scan(step, state0, xs)
    o = o_intra + jnp.moveaxis(o_inter, 0, 2)
    o = o.transpose(0, 2, 3, 1, 4).reshape(B, S, GLA_HEADS, GLA_HV)
    o = o * lax.rsqrt(jnp.mean(o * o, axis=-1, keepdims=True) + EPS)
    o = o * og.astype(jnp.float32).reshape(GLA_HEADS, GLA_HV)
    o = o.reshape(B, S, GLA_DV) * jax.nn.silu(r.astype(jnp.float32))
    return o.astype(dt)


def pool_mixer(u, pw, scale):
    B, S, _ = u.shape
    uf = u.astype(jnp.float32)
    cs = jnp.concatenate([jnp.zeros((B, 1, BRANCH_W), jnp.float32), jnp.cumsum(uf, axis=1)], axis=1)
    hi = jnp.arange(1, S + 1)
    outs = []
    for gi, w in enumerate(POOL_WINDOWS):
        sl = slice(gi * POOL_GW, (gi + 1) * POOL_GW)
        lo = jnp.maximum(hi - w, 0)
        csg = cs[:, :, sl]
        cnt = (hi - lo).astype(jnp.float32)[None, :, None]
        mixed = (csg[:, hi] - csg[:, lo]) / cnt - uf[:, :, sl]
        outs.append(mixed.astype(u.dtype) @ pw[gi])
    return jnp.concatenate(outs, axis=-1) * scale


def hybrid_layer(x, norm1_g, w_in, fox_fb, fox_qg, fox_kg, conv_dw, conv_db, conv_ln_g, conv_ln_b,
                 gla_wa, gla_ba, gla_og, pool_w, pool_scale, gate_b, w_branch, w_out,
                 norm2_g, ffn_up, ffn_dw, ffn_db, ffn_down):
    B, S, D = x.shape
    h = rmsnorm(x, norm1_g)
    z = h @ w_in
    split_at = [int(v) for v in np.cumsum(IN_SIZES)[:-1]]
    fq, fk, fv, ff, cz, gq, gk, gv, ga, gr, pz, gt = jnp.split(z, split_at, axis=-1)
    o_a = fox_mixer(fq, fk, fv, ff, fox_fb, fox_qg, fox_kg)
    o_b = conformer_conv(cz, conv_dw, conv_db, conv_ln_g, conv_ln_b)
    o_c = gla_mixer(gq, gk, gv, ga, gr, gla_wa, gla_ba, gla_og)
    o_d = pool_mixer(pz, pool_w, pool_scale)
    o = jnp.stack([o_a, o_b, o_c, o_d], axis=2)
    br = jnp.einsum('bsnw,nwd->bsnd', o, w_branch)
    gates = jax.nn.sigmoid(gt + gate_b).reshape(B, S, N_BRANCH, D)
    y = jnp.sum(gates * br, axis=2)
    x = x + y @ w_out
    h2 = rmsnorm(x, norm2_g)
    u = causal_dwconv(h2 @ ffn_up, ffn_dw, ffn_db)
    a, v = jnp.split(u, 2, axis=-1)
    return x + (jax.nn.silu(a) * v) @ ffn_down


def setup_inputs(seed: int = 0) -> dict:
    key = jax.random.key(seed)
    ks = jax.random.split(key, 24)
    L, D, f32 = DEPTH, D_MODEL, jnp.float32
    nrm = lambda k, shape, s: jax.random.normal(k, shape, f32) * s
    res = (2.0 * DEPTH) ** -0.5
    return {
        "x": nrm(ks[0], (BATCH, SEQ, D), 1.0),
        "norm1_g": 1.0 + nrm(ks[1], (L, D), 0.02),
        "w_in": nrm(ks[2], (L, D, D_IN), D ** -0.5),
        "fox_fb": 3.0 + nrm(ks[3], (L, FOX_HEADS), 0.1),
        "fox_qg": 1.0 + nrm(ks[4], (L, FOX_HD), 0.02),
        "fox_kg": 1.0 + nrm(ks[5], (L, FOX_HD), 0.02),
        "conv_dw": nrm(ks[6], (L, CONV_K, CONV_W), CONV_K ** -0.5),
        "conv_db": nrm(ks[7], (L, CONV_W), 0.01),
        "conv_ln_g": 1.0 + nrm(ks[8], (L, CONV_W), 0.02),
        "conv_ln_b": nrm(ks[9], (L, CONV_W), 0.01),
        "gla_wa": nrm(ks[10], (L, GLA_RANK, GLA_DK), GLA_RANK ** -0.5),
        "gla_ba": nrm(ks[11], (L, GLA_DK), 0.01),
        "gla_og": 1.0 + nrm(ks[12], (L, GLA_DV), 0.02),
        "pool_w": nrm(ks[13], (L, POOL_GROUPS, POOL_GW, POOL_GW), POOL_GW ** -0.5),
        "pool_scale": 1.0 + nrm(ks[14], (L, BRANCH_W), 0.1),
        "gate_b": nrm(ks[15], (L, N_BRANCH * D), 0.01),
        "w_branch": nrm(ks[16], (L, N_BRANCH, BRANCH_W, D), BRANCH_W ** -0.5),
        "w_out": nrm(ks[17], (L, D, D), D ** -0.5 * res),
        "norm2_g": 1.0 + nrm(ks[18], (L, D), 0.02),
        "ffn_up": nrm(ks[19], (L, D, 2 * D_FF), D ** -0.5),
        "ffn_dw": nrm(ks[20], (L, FFN_K, 2 * D_FF), FFN_K ** -0.5),
        "ffn_db": nrm(ks[21], (L, 2 * D_FF), 0.01),
        "ffn_down": nrm(ks[22], (L, D_FF, D), D_FF ** -0.5 * res),
    }


def reference(x, norm1_g, w_in, fox_fb, fox_qg, fox_kg, conv_dw, conv_db, conv_ln_g, conv_ln_b,
              gla_wa, gla_ba, gla_og, pool_w, pool_scale, gate_b, w_branch, w_out,
              norm2_g, ffn_up, ffn_dw, ffn_db, ffn_down):
    for l in range(DEPTH):
        x = hybrid_layer(x, norm1_g[l], w_in[l], fox_fb[l], fox_qg[l], fox_kg[l],
                         conv_dw[l], conv_db[l], conv_ln_g[l], conv_ln_b[l],
                         gla_wa[l], gla_ba[l], gla_og[l], pool_w[l], pool_scale[l],
                         gate_b[l], w_branch[l], w_out[l], norm2_g[l],
                         ffn_up[l], ffn_dw[l], ffn_db[l], ffn_down[l])
    return x
```

```python
import functools

import jax
import jax.numpy as jnp
from jax import lax
from jax.experimental import pallas as pl
from jax.experimental.pallas import tpu as pltpu

f32 = jnp.float32
MXU_DT = jnp.bfloat16
HI = lax.Precision.HIGHEST

EPS = 1e-6
BRANCH_W = 512
FOX_HEADS = 8
FOX_HD = 64
CONV_K = 31
GLA_HEADS = 4
GLA_DK = 256
GLA_DV = 512
GLA_RANK = 16
GLA_TEMP = 16.0
GLA_CHUNK = 16
POOL_WINDOWS = (2, 4, 8, 16)
FFN_K = 3
LANES = 128

Z_FQ, Z_FK, Z_FV, Z_CA, Z_CG = 0, 512, 1024, 1536, 2048
Z_GQ, Z_GK, Z_GV, Z_GR, Z_PZ, Z_SM, Z_COLS = 2560, 2816, 3072, 3584, 4096, 4608, 5120
SM_FF, SM_GA = 0, 8

NEG_BIG = -1e30
VMEM_LIMIT = 56 * 1024 * 1024


def _cparams(sem):
    return pltpu.CompilerParams(dimension_semantics=sem, vmem_limit_bytes=VMEM_LIMIT)


def _group(idx, size):
    assert size & (size - 1) == 0
    return idx >> (size.bit_length() - 1)


def _log_sigmoid(x):
    return jnp.minimum(x, 0.0) - jnp.log1p(jnp.exp(-jnp.abs(x)))


def _inproj_body(x_ref, g_ref, w_ref, h_ref, z_ref, hs_ref):
    @pl.when(pl.program_id(1) == 0)
    def _():
        x = x_ref[...]
        ms = jnp.mean(x * x, axis=-1, keepdims=True)
        h = (x * lax.rsqrt(ms + EPS) * g_ref[...]).astype(hs_ref.dtype)
        hs_ref[...] = h
        h_ref[...] = h
    z_ref[...] = jnp.dot(hs_ref[...], w_ref[...], preferred_element_type=f32)


def _inproj(x2, g, w_main):
    T, D = x2.shape
    tm, tn = min(512, T), 1024
    return pl.pallas_call(
        _inproj_body,
        grid=(T // tm, Z_COLS // tn),
        in_specs=[pl.BlockSpec((tm, D), lambda i, j: (i, 0)),
                  pl.BlockSpec((1, D), lambda i, j: (0, 0)),
                  pl.BlockSpec((D, tn), lambda i, j: (0, j))],
        out_specs=[pl.BlockSpec((tm, D), lambda i, j: (i, 0)),
                   pl.BlockSpec((tm, tn), lambda i, j: (i, j))],
        out_shape=[jax.ShapeDtypeStruct((T, D), MXU_DT),
                   jax.ShapeDtypeStruct((T, Z_COLS), f32)],
        scratch_shapes=[pltpu.VMEM((tm, D), MXU_DT)],
        compiler_params=_cparams(("parallel", "arbitrary")),
        name="inproj",
    )(x2, g, w_main)


def _head_rmsnorm(x, g, lo):
    sq = x * x
    s0 = jnp.sum(jnp.where(lo, sq, 0.0), axis=-1, keepdims=True)
    s1 = jnp.sum(jnp.where(lo, 0.0, sq), axis=-1, keepdims=True)
    r = jnp.where(lo, lax.rsqrt(s0 * (1.0 / FOX_HD) + EPS), lax.rsqrt(s1 * (1.0 / FOX_HD) + EPS))
    return x * r * g


def _foxprep_body(q_ref, k_ref, v_ref, sm_ref, fb_ref, qg_ref, kg_ref,
                  qn_ref, kn_ref, vb_ref, c_ref, carry_ref, *, tp):
    i = pl.program_id(1)

    @pl.when(i == 0)
    def _():
        carry_ref[...] = jnp.zeros_like(carry_ref)

    lo = lax.broadcasted_iota(jnp.int32, (1, LANES), 1) < FOX_HD
    scale = FOX_HD ** -0.5
    for c in range(BRANCH_W // LANES):
        sl = slice(c * LANES, (c + 1) * LANES)
        qn_ref[:, sl] = (_head_rmsnorm(q_ref[:, sl], qg_ref[:, sl], lo) * scale).astype(qn_ref.dtype)
        kn_ref[:, sl] = _head_rmsnorm(k_ref[:, sl], kg_ref[:, sl], lo).astype(kn_ref.dtype)
    vb_ref[...] = v_ref[...].astype(vb_ref.dtype)

    logf = _log_sigmoid(sm_ref[...] + fb_ref[...])
    row = lax.broadcasted_iota(jnp.int32, (tp, tp), 0)
    col = lax.broadcasted_iota(jnp.int32, (tp, tp), 1)
    tri = (col <= row).astype(f32)
    c = jnp.dot(tri, logf, preferred_element_type=f32, precision=HI) + carry_ref[0:1, :]
    c_ref[...] = c
    carry_ref[0:1, :] = c[tp - 1:tp, :]


def _foxprep(z3, fb_pad, qg, kg):
    B, S, _ = z3.shape
    tp = min(512, S)
    blk = lambda w, c: pl.BlockSpec((None, tp, w), lambda b, i, c=c: (b, i, c))
    vec = lambda w: pl.BlockSpec((1, w), lambda b, i: (0, 0))
    return pl.pallas_call(
        functools.partial(_foxprep_body, tp=tp),
        grid=(B, S // tp),
        in_specs=[blk(BRANCH_W, Z_FQ // BRANCH_W), blk(BRANCH_W, Z_FK // BRANCH_W),
                  blk(BRANCH_W, Z_FV // BRANCH_W), blk(LANES, Z_SM // LANES),
                  vec(LANES), vec(BRANCH_W), vec(BRANCH_W)],
        out_specs=[blk(BRANCH_W, 0), blk(BRANCH_W, 0), blk(BRANCH_W, 0), blk(LANES, 0)],
        out_shape=[jax.ShapeDtypeStruct((B, S, BRANCH_W), MXU_DT)] * 3
        + [jax.ShapeDtypeStruct((B, S, LANES), f32)],
        scratch_shapes=[pltpu.VMEM((8, LANES), f32)],
        compiler_params=_cparams(("parallel", "arbitrary")),
        name="foxprep",
    )(z3, z3, z3, z3, fb_pad, qg, kg)


def _fox_body(q_ref, k_ref, v_ref, ccol_ref, crow_ref, o_ref, m_ref, l_ref, acc_ref, *, tq):
    hp = pl.program_id(1)
    i = pl.program_id(2)
    lane = lax.broadcasted_iota(jnp.int32, (1, LANES), 1)
    lo = lane < FOX_HD
    q = q_ref[...]
    zero = jnp.zeros_like(q)
    qs = (jnp.where(lo, q, zero), jnp.where(lo, zero, q))
    cc = ccol_ref[...]
    ct = tuple(jnp.sum(jnp.where(lane == 2 * hp + e, cc, 0.0), axis=-1, keepdims=True) for e in (0, 1))

    m_ref[...] = jnp.full_like(m_ref, NEG_BIG)
    l_ref[...] = jnp.zeros_like(l_ref)
    acc_ref[...] = jnp.zeros_like(acc_ref)

    def block(j, masked):
        start = pl.multiple_of(j * tq, tq)
        kk = k_ref[pl.ds(start, tq), :]
        vv = v_ref[pl.ds(start, tq), :]
        cr = crow_ref[:, pl.ds(start, tq)]
        alphas, pvs = [], []
        for e in (0, 1):
            s = lax.dot_general(qs[e], kk, (((1,), (1,)), ((), ())), preferred_element_type=f32)
            s = s + ct[e] - cr[e:e + 1, :]
            if masked:
                row = lax.broadcasted_iota(jnp.int32, (tq, tq), 0)
                col = lax.broadcasted_iota(jnp.int32, (tq, tq), 1)
                s = jnp.where(col <= row, s, NEG_BIG)
            m_prev = m_ref[e]
            m_new = jnp.maximum(m_prev, jnp.max(s, axis=-1, keepdims=True))
            alpha = jnp.exp(m_prev - m_new)
            p = jnp.exp(s - m_new)
            l_ref[e] = alpha * l_ref[e] + jnp.sum(p, axis=-1, keepdims=True)
            m_ref[e] = m_new
            alphas.append(alpha)
            pvs.append(jnp.dot(p.astype(vv.dtype), vv, preferred_element_type=f32))
        acc_ref[...] = jnp.where(lo, alphas[0], alphas[1]) * acc_ref[...] + jnp.where(lo, pvs[0], pvs[1])

    def full_block(j, carry):
        block(j, False)
        return carry

    lax.fori_loop(0, i, full_block, 0)
    block(i, True)
    o_ref[...] = (acc_ref[...] / jnp.where(lo, l_ref[0], l_ref[1])).astype(o_ref.dtype)


def _fox_attention(qn, kn, vb, ccol, crow):
    B, S, _ = qn.shape
    tq = min(256, S)
    n_pairs = BRANCH_W // LANES
    return pl.pallas_call(
        functools.partial(_fox_body, tq=tq),
        grid=(B, n_pairs, S // tq),
        in_specs=[pl.BlockSpec((None, tq, LANES), lambda b, h, i: (b, i, h)),
                  pl.BlockSpec((None, S, LANES), lambda b, h, i: (b, 0, h)),
                  pl.BlockSpec((None, S, LANES), lambda b, h, i: (b, 0, h)),
                  pl.BlockSpec((None, tq, LANES), lambda b, h, i: (b, i, 0)),
                  pl.BlockSpec((None, None, 2, S), lambda b, h, i: (b, h, 0, 0))],
        out_specs=pl.BlockSpec((None, tq, LANES), lambda b, h, i: (b, i, h)),
        out_shape=jax.ShapeDtypeStruct((B, S, BRANCH_W), MXU_DT),
        scratch_shapes=[pltpu.VMEM((2, tq, 1), f32), pltpu.VMEM((2, tq, 1), f32),
                        pltpu.VMEM((tq, LANES), f32)],
        compiler_params=_cparams(("parallel", "parallel", "arbitrary")),
        name="fox_attn",
    )(qn, kn, vb, ccol, crow)


CONV_HALO = 32
CONV_ROWS = 64


def _convmix_body(a_ref, g_ref, dw_ref, db_ref, lng_ref, lnb_ref, o_ref, u_ref, *, tp):
    @pl.when(pl.program_id(1) == 0)
    def _():
        u_ref[0:CONV_HALO, :] = jnp.zeros((CONV_HALO, BRANCH_W), f32)

    u_ref[CONV_HALO:CONV_HALO + tp, :] = a_ref[...] * jax.nn.sigmoid(g_ref[...])
    for r in range(0, tp, CONV_ROWS):
        acc = jnp.broadcast_to(db_ref[...], (CONV_ROWS, BRANCH_W))
        for k in range(CONV_K):
            off = r + CONV_HALO - (CONV_K - 1) + k
            acc = acc + dw_ref[k:k + 1, :] * u_ref[off:off + CONV_ROWS, :]
        mu = jnp.mean(acc, axis=-1, keepdims=True)
        d = acc - mu
        var = jnp.mean(d * d, axis=-1, keepdims=True)
        y = d * lax.rsqrt(var + EPS) * lng_ref[...] + lnb_ref[...]
        o_ref[r:r + CONV_ROWS, :] = (y * jax.nn.sigmoid(y)).astype(o_ref.dtype)
    u_ref[0:CONV_HALO, :] = u_ref[tp:tp + CONV_HALO, :]


def _convmix(z3, dw_pad, db, lng, lnb):
    B, S, _ = z3.shape
    tp = min(256, S)
    blk = lambda c: pl.BlockSpec((None, tp, BRANCH_W), lambda b, i, c=c: (b, i, c))
    vec = pl.BlockSpec((1, BRANCH_W), lambda b, i: (0, 0))
    return pl.pallas_call(
        functools.partial(_convmix_body, tp=tp),
        grid=(B, S // tp),
        in_specs=[blk(Z_CA // BRANCH_W), blk(Z_CG // BRANCH_W),
                  pl.BlockSpec(dw_pad.shape, lambda b, i: (0, 0)), vec, vec, vec],
        out_specs=blk(0),
        out_shape=jax.ShapeDtypeStruct((B, S, BRANCH_W), MXU_DT),
        scratch_shapes=[pltpu.VMEM((CONV_HALO + tp, BRANCH_W), f32)],
        compiler_params=_cparams(("parallel", "arbitrary")),
        name="convmix",
    )(z3, z3, dw_pad, db, lng, lnb)


POOL_HALO = 16


def _poolmix_body(u_in_ref, pw_ref, sc_ref, o_ref, u_ref, *, tp):
    i = pl.program_id(1)

    @pl.when(i == 0)
    def _():
        u_ref[0:POOL_HALO, :] = jnp.zeros((POOL_HALO, BRANCH_W), f32)

    u_ref[POOL_HALO:POOL_HALO + tp, :] = u_in_ref[...]
    pos = i * tp + lax.broadcasted_iota(jnp.int32, (tp, 1), 0)
    for gi, w in enumerate(POOL_WINDOWS):
        sl = slice(gi * LANES, (gi + 1) * LANES)
        acc = u_ref[POOL_HALO:POOL_HALO + tp, sl]
        for j in range(1, w):
            acc = acc + u_ref[POOL_HALO - j:POOL_HALO - j + tp, sl]
        cnt = jnp.minimum(pos + 1, w).astype(f32)
        mixed = acc / cnt - u_ref[POOL_HALO:POOL_HALO + tp, sl]
        out = jnp.dot(mixed.astype(pw_ref.dtype), pw_ref[gi], preferred_element_type=f32)
        o_ref[:, sl] = (out * sc_ref[:, sl]).astype(o_ref.dtype)
    u_ref[0:POOL_HALO, :] = u_ref[tp:tp + POOL_HALO, :]


def _poolmix(z3, pw, scale):
    B, S, _ = z3.shape
    tp = min(512, S)
    return pl.pallas_call(
        functools.partial(_poolmix_body, tp=tp),
        grid=(B, S // tp),
        in_specs=[pl.BlockSpec((None, tp, BRANCH_W), lambda b, i: (b, i, Z_PZ // BRANCH_W)),
                  pl.BlockSpec(pw.shape, lambda b, i: (0, 0, 0)),
                  pl.BlockSpec((1, BRANCH_W), lambda b, i: (0, 0))],
        out_specs=pl.BlockSpec((None, tp, BRANCH_W), lambda b, i: (b, i, 0)),
        out_shape=jax.ShapeDtypeStruct((B, S, BRANCH_W), MXU_DT),
        scratch_shapes=[pltpu.VMEM((POOL_HALO + tp, BRANCH_W), f32)],
        compiler_params=_cparams(("parallel", "arbitrary")),
        name="poolmix",
    )(z3, pw, scale)


def _gla_body(q_ref, k_ref, v_ref, r_ref, sm_ref, wa_ref, ba_ref, og_ref, o_ref,
              st_ref, kb_ref, bb_ref, vb_ref, qd_ref, kd_ref, dec_ref, oi_ref, *, tg):
    C = GLA_CHUNK
    hk = GLA_DK // GLA_HEADS
    hv = GLA_DV // GLA_HEADS

    @pl.when(pl.program_id(1) == 0)
    def _():
        st_ref[...] = jnp.zeros_like(st_ref)

    pre = jnp.dot(sm_ref[...].astype(wa_ref.dtype), wa_ref[...], preferred_element_type=f32) + ba_ref[...]
    loga = _log_sigmoid(pre) * (1.0 / GLA_TEMP)
    row = lax.broadcasted_iota(jnp.int32, (tg, tg), 0)
    col = lax.broadcasted_iota(jnp.int32, (tg, tg), 1)
    same = _group(row, C) == _group(col, C)
    bc = jnp.dot((same & (col <= row)).astype(f32), loga, preferred_element_type=f32, precision=HI)
    blast = jnp.dot(same.astype(f32), loga, preferred_element_type=f32, precision=HI)

    q = q_ref[...] * (hk ** -0.5)
    k = k_ref[...]
    qd_ref[...] = (q * jnp.exp(bc)).astype(qd_ref.dtype)
    kd_ref[...] = (k * jnp.exp(blast - bc)).astype(kd_ref.dtype)
    dec_ref[...] = jnp.exp(blast)

    kb_ref[0:C, :] = jnp.zeros((C, GLA_DK), f32)
    bb_ref[0:C, :] = jnp.zeros((C, GLA_DK), f32)
    vb_ref[0:C, :] = jnp.zeros((C, GLA_DV), f32)
    kb_ref[C:C + tg, :] = k
    bb_ref[C:C + tg, :] = bc
    vb_ref[C:C + tg, :] = v_ref[...]
    rb = (_group(lax.broadcasted_iota(jnp.int32, (GLA_DK, GLA_DV), 0), hk)
          == _group(lax.broadcasted_iota(jnp.int32, (GLA_DK, GLA_DV), 1), hv)).astype(MXU_DT)
    rpos = lax.broadcasted_iota(jnp.int32, (tg, 1), 0) & (C - 1)
    o_intra = jnp.zeros((tg, GLA_DV), f32)
    for delta in range(C):
        valid = rpos >= delta
        ks = kb_ref[C - delta:C - delta + tg, :]
        bs = bb_ref[C - delta:C - delta + tg, :]
        vs = vb_ref[C - delta:C - delta + tg, :]
        w = jnp.where(valid, q * ks * jnp.exp(jnp.where(valid, bc - bs, 0.0)), 0.0)
        a = jnp.dot(w.astype(MXU_DT), rb, preferred_element_type=f32)
        o_intra = o_intra + a * vs
    oi_ref[...] = o_intra

    bd = (_group(lax.broadcasted_iota(jnp.int32, (GLA_DV, GLA_DK), 0), hv)
          == _group(lax.broadcasted_iota(jnp.int32, (GLA_DV, GLA_DK), 1), hk))

    def step(n, carry):
        r0 = pl.multiple_of(n * C, C)
        st = st_ref[...]
        o_inter = lax.dot_general(qd_ref[pl.ds(r0, C), :], st.astype(MXU_DT),
                                  (((1,), (1,)), ((), ())), preferred_element_type=f32)
        oi_ref[pl.ds(r0, C), :] += o_inter
        vn = v_ref[pl.ds(r0, C), :].astype(MXU_DT)
        upd = lax.dot_general(vn, kd_ref[pl.ds(r0, C), :], (((0,), (0,)), ((), ())),
                              preferred_element_type=f32)
        st_ref[...] = st * dec_ref[pl.ds(r0, 1), :] + jnp.where(bd, upd, 0.0)
        return carry

    lax.fori_loop(0, tg // C, step, 0)

    for h in range(GLA_HEADS):
        sl = slice(h * hv, (h + 1) * hv)
        oh = oi_ref[:, sl]
        oh = oh * lax.rsqrt(jnp.mean(oh * oh, axis=-1, keepdims=True) + EPS) * og_ref[:, sl]
        rr = r_ref[:, sl]
        o_ref[:, sl] = (oh * (rr * jax.nn.sigmoid(rr))).astype(o_ref.dtype)


def _glamix(z3, wa_pad, ba, og):
    B, S, _ = z3.shape
    tg = min(256, S)
    blk = lambda w, c: pl.BlockSpec((None, tg, w), lambda b, i, c=c: (b, i, c))
    vec = lambda w: pl.BlockSpec((1, w), lambda b, i: (0, 0))
    return pl.pallas_call(
        functools.partial(_gla_body, tg=tg),
        grid=(B, S // tg),
        in_specs=[blk(GLA_DK, Z_GQ // GLA_DK), blk(GLA_DK, Z_GK // GLA_DK),
                  blk(GLA_DV, Z_GV // GLA_DV), blk(GLA_DV, Z_GR // GLA_DV),
                  blk(LANES, Z_SM // LANES),
                  pl.BlockSpec((LANES, GLA_DK), lambda b, i: (0, 0)), vec(GLA_DK), vec(GLA_DV)],
        out_specs=blk(GLA_DV, 0),
        out_shape=jax.ShapeDtypeStruct((B, S, GLA_DV), MXU_DT),
        scratch_shapes=[pltpu.VMEM((GLA_DV, GLA_DK), f32),
                        pltpu.VMEM((GLA_CHUNK + tg, GLA_DK), f32),
                        pltpu.VMEM((GLA_CHUNK + tg, GLA_DK), f32),
                        pltpu.VMEM((GLA_CHUNK + tg, GLA_DV), f32),
                        pltpu.VMEM((tg, GLA_DK), MXU_DT),
                        pltpu.VMEM((tg, GLA_DK), MXU_DT),
                        pltpu.VMEM((tg, GLA_DK), f32),
                        pltpu.VMEM((tg, GLA_DV), f32)],
        compiler_params=_cparams(("parallel", "arbitrary")),
        name="glamix",
    )(z3, z3, z3, z3, z3, wa_pad, ba, og)


def _merge_body(h_ref, oa_ref, ob_ref, oc_ref, od_ref, wg0, wg1, wg2, wg3, gb0, gb1, gb2, gb3,
                p0, p1, p2, p3, y_ref):
    h = h_ref[...]
    acc = None
    for o_ref, wg, gb, p in ((oa_ref, wg0, gb0, p0), (ob_ref, wg1, gb1, p1),
                             (oc_ref, wg2, gb2, p2), (od_ref, wg3, gb3, p3)):
        gate = jax.nn.sigmoid(jnp.dot(h, wg[...], preferred_element_type=f32) + gb[...])
        term = gate * jnp.dot(o_ref[...], p[...], preferred_element_type=f32)
        acc = term if acc is None else acc + term
    y_ref[...] = acc.astype(y_ref.dtype)


def _merge(h, outs, w_gate, gate_b, w_branch):
    T, D = h.shape
    tm, tn = min(1024, T), 512
    nj = D // tn
    tok = lambda w: pl.BlockSpec((tm, w), lambda i, j: (i, 0))
    wg = [pl.BlockSpec((D, tn), lambda i, j, b=b: (0, b * nj + j)) for b in range(4)]
    gb = [pl.BlockSpec((1, tn), lambda i, j, b=b: (0, b * nj + j)) for b in range(4)]
    pb = [pl.BlockSpec((None, BRANCH_W, tn), lambda i, j, b=b: (b, 0, j)) for b in range(4)]
    return pl.pallas_call(
        _merge_body,
        grid=(T // tm, nj),
        in_specs=[tok(D)] + [tok(BRANCH_W)] * 4 + wg + gb + pb,
        out_specs=pl.BlockSpec((tm, tn), lambda i, j: (i, j)),
        out_shape=jax.ShapeDtypeStruct((T, D), MXU_DT),
        compiler_params=_cparams(("parallel", "arbitrary")),
        name="merge",
    )(h, *outs, *([w_gate] * 4), *([gate_b] * 4), *([w_branch] * 4))


def _outproj_body(y_ref, x_ref, w_ref, g_ref, x1_ref, h2_ref):
    x1 = x_ref[...] + jnp.dot(y_ref[...], w_ref[...], preferred_element_type=f32)
    x1_ref[...] = x1
    ms = jnp.mean(x1 * x1, axis=-1, keepdims=True)
    h2_ref[...] = (x1 * lax.rsqrt(ms + EPS) * g_ref[...]).astype(h2_ref.dtype)


def _outproj(y, x2, w_out, g2):
    T, D = x2.shape
    tm = min(512, T)
    tok = pl.BlockSpec((tm, D), lambda i: (i, 0))
    return pl.pallas_call(
        _outproj_body,
        grid=(T // tm,),
        in_specs=[tok, tok, pl.BlockSpec((D, D), lambda i: (0, 0)), pl.BlockSpec((1, D), lambda i: (0, 0))],
        out_specs=[tok, tok],
        out_shape=[jax.ShapeDtypeStruct((T, D), f32), jax.ShapeDtypeStruct((T, D), MXU_DT)],
        compiler_params=_cparams(("parallel",)),
        name="outproj",
    )(y, x2, w_out, g2)


FFN_HALO = 8


def _ffnup_body(h_ref, wa_ref, wv_ref, dwa_ref, dwv_ref, dba_ref, dbv_ref, g_ref, ua_ref, uv_ref,
                *, tm, tiles_per_seq):
    @pl.when(pl.program_id(1) % tiles_per_seq == 0)
    def _():
        ua_ref[0:FFN_HALO, :] = jnp.zeros((FFN_HALO, ua_ref.shape[1]), f32)
        uv_ref[0:FFN_HALO, :] = jnp.zeros((FFN_HALO, uv_ref.shape[1]), f32)

    h = h_ref[...]

    def conv(w_ref, dw_ref, db_ref, u_ref):
        u_ref[FFN_HALO:FFN_HALO + tm, :] = jnp.dot(h, w_ref[...], preferred_element_type=f32)
        y = db_ref[...] + dw_ref[FFN_K - 1:FFN_K, :] * u_ref[FFN_HALO:FFN_HALO + tm, :]
        for k in range(FFN_K - 1):
            off = FFN_HALO - (FFN_K - 1) + k
            y = y + dw_ref[k:k + 1, :] * u_ref[off:off + tm, :]
        u_ref[0:FFN_HALO, :] = u_ref[tm:tm + FFN_HALO, :]
        return y

    a = conv(wa_ref, dwa_ref, dba_ref, ua_ref)
    v = conv(wv_ref, dwv_ref, dbv_ref, uv_ref)
    g_ref[...] = (a * jax.nn.sigmoid(a) * v).astype(g_ref.dtype)


def _ffnup(h2, ffn_up, ffn_dw, ffn_db, seq_len):
    T, D = h2.shape
    dff = ffn_up.shape[1] // 2
    tm, tn = min(1024, seq_len), 512
    nj = dff // tn
    return pl.pallas_call(
        functools.partial(_ffnup_body, tm=tm, tiles_per_seq=seq_len // tm),
        grid=(nj, T // tm),
        in_specs=[pl.BlockSpec((tm, D), lambda j, i: (i, 0)),
                  pl.BlockSpec((D, tn), lambda j, i: (0, j)),
                  pl.BlockSpec((D, tn), lambda j, i: (0, nj + j)),
                  pl.BlockSpec((FFN_K, tn), lambda j, i: (0, j)),
                  pl.BlockSpec((FFN_K, tn), lambda j, i: (0, nj + j)),
                  pl.BlockSpec((1, tn), lambda j, i: (0, j)),
                  pl.BlockSpec((1, tn), lambda j, i: (0, nj + j))],
        out_specs=pl.BlockSpec((tm, tn), lambda j, i: (i, j)),
        out_shape=jax.ShapeDtypeStruct((T, dff), MXU_DT),
        scratch_shapes=[pltpu.VMEM((FFN_HALO + tm, tn), f32), pltpu.VMEM((FFN_HALO + tm, tn), f32)],
        compiler_params=_cparams(("parallel", "arbitrary")),
        name="ffnup",
    )(h2, ffn_up, ffn_up, ffn_dw, ffn_dw, ffn_db, ffn_db)


def _ffndown_body(g_ref, w_ref, x_ref, o_ref):
    @pl.when(pl.program_id(1) == 0)
    def _():
        o_ref[...] = x_ref[...]
    o_ref[...] += jnp.dot(g_ref[...], w_ref[...], preferred_element_type=f32)


def _ffndown(g, ffn_down, x1):
    T, D = x1.shape
    dff = g.shape[1]
    tm, tk = min(1024, T), 512
    return pl.pallas_call(
        _ffndown_body,
        grid=(T // tm, dff // tk),
        in_specs=[pl.BlockSpec((tm, tk), lambda i, k: (i, k)),
                  pl.BlockSpec((tk, D), lambda i, k: (k, 0)),
                  pl.BlockSpec((tm, D), lambda i, k: (i, 0))],
        out_specs=pl.BlockSpec((tm, D), lambda i, k: (i, 0)),
        out_shape=jax.ShapeDtypeStruct((T, D), f32),
        compiler_params=_cparams(("parallel", "arbitrary")),
        name="ffndown",
    )(g, ffn_down, x1)


def _mixers(z3, p):
    B, S, _ = z3.shape
    qn, kn, vb, ccol = _foxprep(z3, p["fox_fb"], p["fox_qg"], p["fox_kg"])
    crow = jnp.transpose(ccol[:, :, :FOX_HEADS], (0, 2, 1)).reshape(B, FOX_HEADS // 2, 2, S)
    o_a = _fox_attention(qn, kn, vb, ccol, crow)
    o_b = _convmix(z3, p["conv_dw"], p["conv_db"], p["conv_ln_g"], p["conv_ln_b"])
    o_c = _glamix(z3, p["gla_wa"], p["gla_ba"], p["gla_og"])
    o_d = _poolmix(z3, p["pool_w"], p["pool_scale"])
    return o_a, o_b, o_c, o_d


def _layer(x, p):
    B, S, D = x.shape
    T = B * S
    x2 = x.reshape(T, D)
    h, z = _inproj(x2, p["norm1_g"], p["w_main"])
    outs = _mixers(z.reshape(B, S, Z_COLS), p)
    y = _merge(h, [o.reshape(T, BRANCH_W) for o in outs], p["w_gate"], p["gate_b"], p["w_branch"])
    x1, h2 = _outproj(y, x2, p["w_out"], p["norm2_g"])
    g = _ffnup(h2, p["ffn_up"], p["ffn_dw"], p["ffn_db"], S)
    return _ffndown(g, p["ffn_down"], x1).reshape(B, S, D)


def _prepare(norm1_g, w_in, fox_fb, fox_qg, fox_kg, conv_dw, conv_db, conv_ln_g, conv_ln_b,
             gla_wa, gla_ba, gla_og, pool_w, pool_scale, gate_b, w_branch, w_out,
             norm2_g, ffn_up, ffn_dw, ffn_db, ffn_down):
    L, D, _ = w_in.shape
    n_small = 3 * BRANCH_W + FOX_HEADS + 2 * BRANCH_W + 2 * GLA_DK + GLA_DV + GLA_RANK + GLA_DV + BRANCH_W
    o_ff = 3 * BRANCH_W
    o_cz = o_ff + FOX_HEADS
    o_gq = o_cz + 2 * BRANCH_W
    o_ga = o_gq + 2 * GLA_DK + GLA_DV
    o_gr = o_ga + GLA_RANK
    o_pz = o_gr + GLA_DV
    w_main = jnp.concatenate(
        [w_in[:, :, 0:o_ff], w_in[:, :, o_cz:o_ga], w_in[:, :, o_gr:n_small],
         w_in[:, :, o_ff:o_cz], w_in[:, :, o_ga:o_gr],
         jnp.zeros((L, D, Z_COLS - n_small), w_in.dtype)], axis=2).astype(MXU_DT)
    row = lambda a: a[:, None, :]
    pad_lanes = lambda a, off: jnp.pad(a, ((0, 0), (off, LANES - off - a.shape[1])))
    wa_pad = jnp.pad(gla_wa, ((0, 0), (SM_GA, LANES - SM_GA - GLA_RANK), (0, 0)))
    return {
        "norm1_g": row(norm1_g),
        "w_main": w_main,
        "w_gate": w_in[:, :, n_small:].astype(MXU_DT),
        "fox_fb": row(pad_lanes(fox_fb, SM_FF)),
        "fox_qg": row(jnp.tile(fox_qg, (1, FOX_HEADS))),
        "fox_kg": row(jnp.tile(fox_kg, (1, FOX_HEADS))),
        "conv_dw": jnp.pad(conv_dw, ((0, 0), (0, 1), (0, 0))),
        "conv_db": row(conv_db), "conv_ln_g": row(conv_ln_g), "conv_ln_b": row(conv_ln_b),
        "gla_wa": wa_pad.astype(MXU_DT), "gla_ba": row(gla_ba), "gla_og": row(gla_og),
        "pool_w": pool_w.astype(MXU_DT), "pool_scale": row(pool_scale),
        "gate_b": row(gate_b),
        "w_branch": w_branch.astype(MXU_DT),
        "w_out": w_out.astype(MXU_DT),
        "norm2_g": row(norm2_g),
        "ffn_up": ffn_up.astype(MXU_DT), "ffn_dw": ffn_dw, "ffn_db": row(ffn_db),
        "ffn_down": ffn_down.astype(MXU_DT),
    }


def kernel(x, norm1_g, w_in, fox_fb, fox_qg, fox_kg, conv_dw, conv_db, conv_ln_g, conv_ln_b, gla_wa, gla_ba, gla_og, pool_w, pool_scale, gate_b, w_branch, w_out, norm2_g, ffn_up, ffn_dw, ffn_db, ffn_down):
    params = _prepare(norm1_g, w_in, fox_fb, fox_qg, fox_kg, conv_dw, conv_db, conv_ln_g, conv_ln_b,
                      gla_wa, gla_ba, gla_og, pool_w, pool_scale, gate_b, w_branch, w_out,
                      norm2_g, ffn_up, ffn_dw, ffn_db, ffn_down)
    for l in range(w_in.shape[0]):
        x = _layer(x, {k: v[l] for k, v in params.items()})
    return x
```

```python
import functools

import jax
import jax.numpy as jnp
from jax import lax
from jax.experimental import pallas as pl
from jax.experimental.pallas import tpu as pltpu

f32 = jnp.float32
MXU_DT = jnp.bfloat16
HI = lax.Precision.HIGHEST

EPS = 1e-6
BRANCH_W = 512
FOX_HEADS = 8
FOX_HD = 64
CONV_K = 31
GLA_HEADS = 4
GLA_DK = 256
GLA_DV = 512
GLA_RANK = 16
GLA_TEMP = 16.0
GLA_CHUNK = 16
POOL_WINDOWS = (2, 4, 8, 16)
FFN_K = 3
LANES = 128

Z_FQ, Z_FK, Z_FV, Z_CA, Z_CG = 0, 512, 1024, 1536, 2048
Z_GQ, Z_GK, Z_GV, Z_GR, Z_PZ, Z_SM, Z_COLS = 2560, 2816, 3072, 3584, 4096, 4608, 5120
SM_FF, SM_GA = 0, 8

NEG_BIG = -1e30
VMEM_LIMIT = 56 * 1024 * 1024


def _cparams(sem):
    return pltpu.CompilerParams(dimension_semantics=sem, vmem_limit_bytes=VMEM_LIMIT)


def _group(idx, size):
    assert size & (size - 1) == 0
    return idx >> (size.bit_length() - 1)


def _log_sigmoid(x):
    return jnp.minimum(x, 0.0) - jnp.log1p(jnp.exp(-jnp.abs(x)))


def _inproj_body(x_ref, g_ref, w_ref, h_ref, z_ref, hs_ref):
    @pl.when(pl.program_id(1) == 0)
    def _():
        x = x_ref[...]
        ms = jnp.mean(x * x, axis=-1, keepdims=True)
        h = (x * lax.rsqrt(ms + EPS) * g_ref[...]).astype(hs_ref.dtype)
        hs_ref[...] = h
        h_ref[...] = h
    z_ref[...] = jnp.dot(hs_ref[...], w_ref[...], preferred_element_type=f32)


def _inproj(x2, g, w_main, l):
    T, D = x2.shape
    tm, tn = min(512, T), 1024
    return pl.pallas_call(
        _inproj_body,
        grid=(T // tm, Z_COLS // tn),
        in_specs=[pl.BlockSpec((tm, D), lambda i, j: (i, 0)),
                  pl.BlockSpec((1, D), lambda i, j: (0, 0)),
                  pl.BlockSpec((None, D, tn), lambda i, j: (l, 0, j))],
        out_specs=[pl.BlockSpec((tm, D), lambda i, j: (i, 0)),
                   pl.BlockSpec((tm, tn), lambda i, j: (i, j))],
        out_shape=[jax.ShapeDtypeStruct((T, D), MXU_DT),
                   jax.ShapeDtypeStruct((T, Z_COLS), f32)],
        scratch_shapes=[pltpu.VMEM((tm, D), MXU_DT)],
        compiler_params=_cparams(("parallel", "arbitrary")),
        name="inproj",
    )(x2, g, w_main)


def _head_rmsnorm(x, g, lo):
    sq = x * x
    s0 = jnp.sum(jnp.where(lo, sq, 0.0), axis=-1, keepdims=True)
    s1 = jnp.sum(jnp.where(lo, 0.0, sq), axis=-1, keepdims=True)
    r = jnp.where(lo, lax.rsqrt(s0 * (1.0 / FOX_HD) + EPS), lax.rsqrt(s1 * (1.0 / FOX_HD) + EPS))
    return x * r * g


N_SPLIT = 3
L_KC = FOX_HD
L_QC = FOX_HD + N_SPLIT
L_ONE = FOX_HD


def _split_select(x, lane, first, sign):
    out = jnp.zeros_like(x)
    rest = x * sign
    for j in range(N_SPLIT):
        piece = rest.astype(jnp.bfloat16).astype(f32)
        out = jnp.where(lane == first + j, piece, out)
        rest = rest - piece
    return out


def _foxprep_body(q_ref, k_ref, v_ref, sm_ref, fb_ref, qg_ref, kg_ref,
                  qt_ref, kt_ref, vt_ref, carry_ref, *, tp):
    i = pl.program_id(1)

    @pl.when(i == 0)
    def _():
        carry_ref[...] = jnp.zeros_like(carry_ref)

    logf = _log_sigmoid(sm_ref[...] + fb_ref[...])
    row = lax.broadcasted_iota(jnp.int32, (tp, tp), 0)
    col = lax.broadcasted_iota(jnp.int32, (tp, tp), 1)
    tri = (col <= row).astype(f32)
    c = jnp.dot(tri, logf, preferred_element_type=f32, precision=HI) + carry_ref[0:1, :]
    carry_ref[0:1, :] = c[tp - 1:tp, :]

    lane = lax.broadcasted_iota(jnp.int32, (1, LANES), 1)
    lo = lane < FOX_HD
    ones_q = ((lane >= L_KC) & (lane < L_KC + N_SPLIT)).astype(f32)
    ones_k = ((lane >= L_QC) & (lane < L_QC + N_SPLIT)).astype(f32)
    ones_v = (lane == L_ONE).astype(f32)
    scale = FOX_HD ** -0.5
    for pair in range(BRANCH_W // LANES):
        sl = slice(pair * LANES, (pair + 1) * LANES)
        qn = _head_rmsnorm(q_ref[:, sl], qg_ref[:, sl], lo) * scale
        kn = _head_rmsnorm(k_ref[:, sl], kg_ref[:, sl], lo)
        vv = v_ref[:, sl]
        for e in (0, 1):
            h = 2 * pair + e
            if e == 1:
                qn, kn, vv = (pltpu.roll(t, FOX_HD, axis=1) for t in (qn, kn, vv))
            cb = jnp.broadcast_to(c[:, h:h + 1], (tp, LANES))
            qt_ref[h] = jnp.where(lo, qn, _split_select(cb, lane, L_QC, 1.0) + ones_q).astype(qt_ref.dtype)
            kt_ref[h] = jnp.where(lo, kn, _split_select(cb, lane, L_KC, -1.0) + ones_k).astype(kt_ref.dtype)
            vt_ref[h] = jnp.where(lo, vv, ones_v).astype(vt_ref.dtype)


def _foxprep(z3, fb_pad, qg, kg):
    B, S, _ = z3.shape
    tp = min(512, S)
    blk = lambda w, c: pl.BlockSpec((None, tp, w), lambda b, i, c=c: (b, i, c))
    vec = lambda w: pl.BlockSpec((1, w), lambda b, i: (0, 0))
    head_blk = pl.BlockSpec((None, FOX_HEADS, tp, LANES), lambda b, i: (b, 0, i, 0))
    return pl.pallas_call(
        functools.partial(_foxprep_body, tp=tp),
        grid=(B, S // tp),
        in_specs=[blk(BRANCH_W, Z_FQ // BRANCH_W), blk(BRANCH_W, Z_FK // BRANCH_W),
                  blk(BRANCH_W, Z_FV // BRANCH_W), blk(LANES, Z_SM // LANES),
                  vec(LANES), vec(BRANCH_W), vec(BRANCH_W)],
        out_specs=[head_blk] * 3,
        out_shape=[jax.ShapeDtypeStruct((B, FOX_HEADS, S, LANES), MXU_DT)] * 3,
        scratch_shapes=[pltpu.VMEM((8, LANES), f32)],
        compiler_params=_cparams(("parallel", "arbitrary")),
        name="foxprep",
    )(z3, z3, z3, z3, fb_pad, qg, kg)


def _fox_body(q_ref, k_ref, v_ref, o_ref, m_ref, acc_ref, *, tq):
    i = pl.program_id(2)
    m_ref[...] = jnp.full_like(m_ref, NEG_BIG)
    acc_ref[...] = jnp.zeros_like(acc_ref)

    heads = range(FOX_GROUP)

    def blocks(j, masked):
        start = pl.multiple_of(j * tq, tq)
        s = [lax.dot_general(k_ref[g, pl.ds(start, tq), :], q_ref[g], (((1,), (1,)), ((), ())),
                             preferred_element_type=f32) for g in heads]
        if masked:
            key = lax.broadcasted_iota(jnp.int32, (tq, tq), 0)
            qry = lax.broadcasted_iota(jnp.int32, (tq, tq), 1)
            s = [jnp.where(key <= qry, sg, NEG_BIG) for sg in s]
        m_prev = [m_ref[g, 0:1, :] for g in heads]
        m_new = [jnp.maximum(m_prev[g], jnp.max(s[g], axis=0, keepdims=True)) for g in heads]
        p = [jnp.exp(s[g] - m_new[g]).astype(q_ref.dtype) for g in heads]
        pv = [jnp.dot(v_ref[g, :, pl.ds(start, tq)], p[g], preferred_element_type=f32) for g in heads]
        for g in heads:
            m_ref[g, 0:1, :] = m_new[g]
            acc_ref[g] = jnp.exp(m_prev[g] - m_new[g]) * acc_ref[g] + pv[g]

    def full_blocks(j, carry):
        blocks(j, False)
        return carry

    lax.fori_loop(0, i, full_blocks, 0)
    blocks(i, True)
    for g in heads:
        o_ref[g] = (acc_ref[g, 0:FOX_HD, :] / acc_ref[g, L_ONE:L_ONE + 1, :]).astype(o_ref.dtype)


FOX_GROUP = 8


def _fox_attention(qt, kt, vtt):
    B, H, S, _ = qt.shape
    tq = min(256, S)
    G = FOX_GROUP
    resident = dict(pipeline_mode=pl.Buffered(1))
    return pl.pallas_call(
        functools.partial(_fox_body, tq=tq),
        grid=(B, H // G, S // tq),
        in_specs=[pl.BlockSpec((None, G, tq, LANES), lambda b, h, i: (b, h, i, 0)),
                  pl.BlockSpec((None, G, S, LANES), lambda b, h, i: (b, h, 0, 0), **resident),
                  pl.BlockSpec((None, G, LANES, S), lambda b, h, i: (b, h, 0, 0), **resident)],
        out_specs=pl.BlockSpec((None, G, FOX_HD, tq), lambda b, h, i: (b, h, 0, i)),
        out_shape=jax.ShapeDtypeStruct((B, H, FOX_HD, S), MXU_DT),
        scratch_shapes=[pltpu.VMEM((G, 8, tq), f32), pltpu.VMEM((G, LANES, tq), f32)],
        compiler_params=_cparams(("parallel", "parallel", "arbitrary")),
        name="fox_attn",
    )(qt, kt, vtt)


CONV_HALO = 32
CONV_ROWS = 64


def _convmix_body(a_ref, g_ref, dw_ref, db_ref, lng_ref, lnb_ref, o_ref, u_ref, *, tp):
    @pl.when(pl.program_id(1) == 0)
    def _():
        u_ref[0:CONV_HALO, :] = jnp.zeros((CONV_HALO, BRANCH_W), f32)

    u_ref[CONV_HALO:CONV_HALO + tp, :] = a_ref[...] * jax.nn.sigmoid(g_ref[...])
    for r in range(0, tp, CONV_ROWS):
        acc = jnp.broadcast_to(db_ref[...], (CONV_ROWS, BRANCH_W))
        for k in range(CONV_K):
            off = r + CONV_HALO - (CONV_K - 1) + k
            acc = acc + dw_ref[k:k + 1, :] * u_ref[off:off + CONV_ROWS, :]
        mu = jnp.mean(acc, axis=-1, keepdims=True)
        d = acc - mu
        var = jnp.mean(d * d, axis=-1, keepdims=True)
        y = d * lax.rsqrt(var + EPS) * lng_ref[...] + lnb_ref[...]
        o_ref[r:r + CONV_ROWS, :] = (y * jax.nn.sigmoid(y)).astype(o_ref.dtype)
    u_ref[0:CONV_HALO, :] = u_ref[tp:tp + CONV_HALO, :]


def _convmix(z3, dw_pad, db, lng, lnb):
    B, S, _ = z3.shape
    tp = min(256, S)
    blk = lambda c: pl.BlockSpec((None, tp, BRANCH_W), lambda b, i, c=c: (b, i, c))
    vec = pl.BlockSpec((1, BRANCH_W), lambda b, i: (0, 0))
    return pl.pallas_call(
        functools.partial(_convmix_body, tp=tp),
        grid=(B, S // tp),
        in_specs=[blk(Z_CA // BRANCH_W), blk(Z_CG // BRANCH_W),
                  pl.BlockSpec(dw_pad.shape, lambda b, i: (0, 0)), vec, vec, vec],
        out_specs=blk(0),
        out_shape=jax.ShapeDtypeStruct((B, S, BRANCH_W), MXU_DT),
        scratch_shapes=[pltpu.VMEM((CONV_HALO + tp, BRANCH_W), f32)],
        compiler_params=_cparams(("parallel", "arbitrary")),
        name="convmix",
    )(z3, z3, dw_pad, db, lng, lnb)


POOL_HALO = 16


def _poolmix_body(u_in_ref, pw_ref, sc_ref, o_ref, u_ref, *, tp):
    i = pl.program_id(1)

    @pl.when(i == 0)
    def _():
        u_ref[0:POOL_HALO, :] = jnp.zeros((POOL_HALO, BRANCH_W), f32)

    u_ref[POOL_HALO:POOL_HALO + tp, :] = u_in_ref[...]
    pos = i * tp + lax.broadcasted_iota(jnp.int32, (tp, 1), 0)
    for gi, w in enumerate(POOL_WINDOWS):
        sl = slice(gi * LANES, (gi + 1) * LANES)
        acc = u_ref[POOL_HALO:POOL_HALO + tp, sl]
        for j in range(1, w):
            acc = acc + u_ref[POOL_HALO - j:POOL_HALO - j + tp, sl]
        cnt = jnp.minimum(pos + 1, w).astype(f32)
        mixed = acc / cnt - u_ref[POOL_HALO:POOL_HALO + tp, sl]
        out = jnp.dot(mixed.astype(pw_ref.dtype), pw_ref[gi], preferred_element_type=f32)
        o_ref[:, sl] = (out * sc_ref[:, sl]).astype(o_ref.dtype)
    u_ref[0:POOL_HALO, :] = u_ref[tp:tp + POOL_HALO, :]


def _poolmix(z3, pw, scale):
    B, S, _ = z3.shape
    tp = min(512, S)
    return pl.pallas_call(
        functools.partial(_poolmix_body, tp=tp),
        grid=(B, S // tp),
        in_specs=[pl.BlockSpec((None, tp, BRANCH_W), lambda b, i: (b, i, Z_PZ // BRANCH_W)),
                  pl.BlockSpec(pw.shape, lambda b, i: (0, 0, 0)),
                  pl.BlockSpec((1, BRANCH_W), lambda b, i: (0, 0))],
        out_specs=pl.BlockSpec((None, tp, BRANCH_W), lambda b, i: (b, i, 0)),
        out_shape=jax.ShapeDtypeStruct((B, S, BRANCH_W), MXU_DT),
        scratch_shapes=[pltpu.VMEM((POOL_HALO + tp, BRANCH_W), f32)],
        compiler_params=_cparams(("parallel", "arbitrary")),
        name="poolmix",
    )(z3, pw, scale)


def _gla_body(q_ref, k_ref, v_ref, r_ref, sm_ref, wa_ref, ba_ref, og_ref, o_ref,
              st_ref, kb_ref, bb_ref, vb_ref, qd_ref, kd_ref, dec_ref, oi_ref, *, tg):
    C = GLA_CHUNK
    hk = GLA_DK // GLA_HEADS
    hv = GLA_DV // GLA_HEADS

    @pl.when(pl.program_id(1) == 0)
    def _():
        st_ref[...] = jnp.zeros_like(st_ref)

    pre = jnp.dot(sm_ref[...].astype(wa_ref.dtype), wa_ref[...], preferred_element_type=f32) + ba_ref[...]
    loga = _log_sigmoid(pre) * (1.0 / GLA_TEMP)
    row = lax.broadcasted_iota(jnp.int32, (tg, tg), 0)
    col = lax.broadcasted_iota(jnp.int32, (tg, tg), 1)
    same = _group(row, C) == _group(col, C)
    bc = jnp.dot((same & (col <= row)).astype(f32), loga, preferred_element_type=f32, precision=HI)
    blast = jnp.dot(same.astype(f32), loga, preferred_element_type=f32, precision=HI)

    q = q_ref[...] * (hk ** -0.5)
    k = k_ref[...]
    qd_ref[...] = (q * jnp.exp(bc)).astype(qd_ref.dtype)
    kd_ref[...] = (k * jnp.exp(blast - bc)).astype(kd_ref.dtype)
    dec_ref[...] = jnp.exp(blast)

    kb_ref[0:C, :] = jnp.zeros((C, GLA_DK), f32)
    bb_ref[0:C, :] = jnp.zeros((C, GLA_DK), f32)
    vb_ref[0:C, :] = jnp.zeros((C, GLA_DV), f32)
    kb_ref[C:C + tg, :] = k
    bb_ref[C:C + tg, :] = bc
    vb_ref[C:C + tg, :] = v_ref[...]
    rb = (_group(lax.broadcasted_iota(jnp.int32, (GLA_DK, GLA_DV), 0), hk)
          == _group(lax.broadcasted_iota(jnp.int32, (GLA_DK, GLA_DV), 1), hv)).astype(MXU_DT)
    rpos = lax.broadcasted_iota(jnp.int32, (tg, 1), 0) & (C - 1)
    o_intra = jnp.zeros((tg, GLA_DV), f32)
    for delta in range(C):
        valid = rpos >= delta
        ks = kb_ref[C - delta:C - delta + tg, :]
        bs = bb_ref[C - delta:C - delta + tg, :]
        vs = vb_ref[C - delta:C - delta + tg, :]
        w = jnp.where(valid, q * ks * jnp.exp(jnp.where(valid, bc - bs, 0.0)), 0.0)
        a = jnp.dot(w.astype(MXU_DT), rb, preferred_element_type=f32)
        o_intra = o_intra + a * vs
    oi_ref[...] = o_intra

    bd = (_group(lax.broadcasted_iota(jnp.int32, (GLA_DV, GLA_DK), 0), hv)
          == _group(lax.broadcasted_iota(jnp.int32, (GLA_DV, GLA_DK), 1), hk))

    def step(n, carry):
        r0 = pl.multiple_of(n * C, C)
        st = st_ref[...]
        o_inter = lax.dot_general(qd_ref[pl.ds(r0, C), :], st.astype(MXU_DT),
                                  (((1,), (1,)), ((), ())), preferred_element_type=f32)
        oi_ref[pl.ds(r0, C), :] += o_inter
        vn = v_ref[pl.ds(r0, C), :].astype(MXU_DT)
        upd = lax.dot_general(vn, kd_ref[pl.ds(r0, C), :], (((0,), (0,)), ((), ())),
                              preferred_element_type=f32)
        st_ref[...] = st * dec_ref[pl.ds(r0, 1), :] + jnp.where(bd, upd, 0.0)
        return carry

    lax.fori_loop(0, tg // C, step, 0)

    for h in range(GLA_HEADS):
        sl = slice(h * hv, (h + 1) * hv)
        oh = oi_ref[:, sl]
        oh = oh * lax.rsqrt(jnp.mean(oh * oh, axis=-1, keepdims=True) + EPS) * og_ref[:, sl]
        rr = r_ref[:, sl]
        o_ref[:, sl] = (oh * (rr * jax.nn.sigmoid(rr))).astype(o_ref.dtype)


def _glamix(z3, wa_pad, ba, og):
    B, S, _ = z3.shape
    tg = min(256, S)
    blk = lambda w, c: pl.BlockSpec((None, tg, w), lambda b, i, c=c: (b, i, c))
    vec = lambda w: pl.BlockSpec((1, w), lambda b, i: (0, 0))
    return pl.pallas_call(
        functools.partial(_gla_body, tg=tg),
        grid=(B, S // tg),
        in_specs=[blk(GLA_DK, Z_GQ // GLA_DK), blk(GLA_DK, Z_GK // GLA_DK),
                  blk(GLA_DV, Z_GV // GLA_DV), blk(GLA_DV, Z_GR // GLA_DV),
                  blk(LANES, Z_SM // LANES),
                  pl.BlockSpec((LANES, GLA_DK), lambda b, i: (0, 0)), vec(GLA_DK), vec(GLA_DV)],
        out_specs=blk(GLA_DV, 0),
        out_shape=jax.ShapeDtypeStruct((B, S, GLA_DV), MXU_DT),
        scratch_shapes=[pltpu.VMEM((GLA_DV, GLA_DK), f32),
                        pltpu.VMEM((GLA_CHUNK + tg, GLA_DK), f32),
                        pltpu.VMEM((GLA_CHUNK + tg, GLA_DK), f32),
                        pltpu.VMEM((GLA_CHUNK + tg, GLA_DV), f32),
                        pltpu.VMEM((tg, GLA_DK), MXU_DT),
                        pltpu.VMEM((tg, GLA_DK), MXU_DT),
                        pltpu.VMEM((tg, GLA_DK), f32),
                        pltpu.VMEM((tg, GLA_DV), f32)],
        compiler_params=_cparams(("parallel", "arbitrary")),
        name="glamix",
    )(z3, z3, z3, z3, z3, wa_pad, ba, og)


def _merge_body(h_ref, oa_ref, ob_ref, oc_ref, od_ref, wg0, wg1, wg2, wg3, gb0, gb1, gb2, gb3,
                p0, p1, p2, p3, y_ref):
    h = h_ref[...]
    acc = None
    for o_ref, wg, gb, p in ((oa_ref, wg0, gb0, p0), (ob_ref, wg1, gb1, p1),
                             (oc_ref, wg2, gb2, p2), (od_ref, wg3, gb3, p3)):
        gate = jax.nn.sigmoid(jnp.dot(h, wg[...], preferred_element_type=f32) + gb[...])
        term = gate * jnp.dot(o_ref[...], p[...], preferred_element_type=f32)
        acc = term if acc is None else acc + term
    y_ref[...] = acc.astype(y_ref.dtype)


def _merge(h, outs, w_gate, gate_b, w_branch, l):
    T, D = h.shape
    tm, tn = min(1024, T), 512
    nj = D // tn
    tok = lambda w: pl.BlockSpec((tm, w), lambda i, j: (i, 0))
    wg = [pl.BlockSpec((None, D, tn), lambda i, j, b=b: (l, 0, b * nj + j)) for b in range(4)]
    gb = [pl.BlockSpec((1, tn), lambda i, j, b=b: (0, b * nj + j)) for b in range(4)]
    pb = [pl.BlockSpec((None, None, BRANCH_W, tn), lambda i, j, b=b: (l, b, 0, j)) for b in range(4)]
    return pl.pallas_call(
        _merge_body,
        grid=(T // tm, nj),
        in_specs=[tok(D)] + [tok(BRANCH_W)] * 4 + wg + gb + pb,
        out_specs=pl.BlockSpec((tm, tn), lambda i, j: (i, j)),
        out_shape=jax.ShapeDtypeStruct((T, D), MXU_DT),
        compiler_params=_cparams(("parallel", "arbitrary")),
        name="merge",
    )(h, *outs, *([w_gate] * 4), *([gate_b] * 4), *([w_branch] * 4))


def _outproj_body(y_ref, x_ref, w_ref, g_ref, x1_ref, h2_ref):
    x1 = x_ref[...] + jnp.dot(y_ref[...], w_ref[...], preferred_element_type=f32)
    x1_ref[...] = x1
    ms = jnp.mean(x1 * x1, axis=-1, keepdims=True)
    h2_ref[...] = (x1 * lax.rsqrt(ms + EPS) * g_ref[...]).astype(h2_ref.dtype)


def _outproj(y, x2, w_out, g2, l):
    T, D = x2.shape
    tm = min(512, T)
    tok = pl.BlockSpec((tm, D), lambda i: (i, 0))
    return pl.pallas_call(
        _outproj_body,
        grid=(T // tm,),
        in_specs=[tok, tok, pl.BlockSpec((None, D, D), lambda i: (l, 0, 0)),
                  pl.BlockSpec((1, D), lambda i: (0, 0))],
        out_specs=[tok, tok],
        out_shape=[jax.ShapeDtypeStruct((T, D), f32), jax.ShapeDtypeStruct((T, D), MXU_DT)],
        compiler_params=_cparams(("parallel",)),
        name="outproj",
    )(y, x2, w_out, g2)


FFN_HALO = 8


def _ffnup_body(h_ref, wa_ref, wv_ref, dwa_ref, dwv_ref, dba_ref, dbv_ref, g_ref, ua_ref, uv_ref,
                *, tm, tiles_per_seq):
    @pl.when(pl.program_id(1) % tiles_per_seq == 0)
    def _():
        ua_ref[0:FFN_HALO, :] = jnp.zeros((FFN_HALO, ua_ref.shape[1]), f32)
        uv_ref[0:FFN_HALO, :] = jnp.zeros((FFN_HALO, uv_ref.shape[1]), f32)

    h = h_ref[...]

    def conv(w_ref, dw_ref, db_ref, u_ref):
        u_ref[FFN_HALO:FFN_HALO + tm, :] = jnp.dot(h, w_ref[...], preferred_element_type=f32)
        y = db_ref[...] + dw_ref[FFN_K - 1:FFN_K, :] * u_ref[FFN_HALO:FFN_HALO + tm, :]
        for k in range(FFN_K - 1):
            off = FFN_HALO - (FFN_K - 1) + k
            y = y + dw_ref[k:k + 1, :] * u_ref[off:off + tm, :]
        u_ref[0:FFN_HALO, :] = u_ref[tm:tm + FFN_HALO, :]
        return y

    a = conv(wa_ref, dwa_ref, dba_ref, ua_ref)
    v = conv(wv_ref, dwv_ref, dbv_ref, uv_ref)
    g_ref[...] = (a * jax.nn.sigmoid(a) * v).astype(g_ref.dtype)


def _ffnup(h2, ffn_up, ffn_dw, ffn_db, seq_len, l):
    T, D = h2.shape
    dff = ffn_up.shape[2] // 2
    tm, tn = min(1024, seq_len), 512
    nj = dff // tn
    return pl.pallas_call(
        functools.partial(_ffnup_body, tm=tm, tiles_per_seq=seq_len // tm),
        grid=(nj, T // tm),
        in_specs=[pl.BlockSpec((tm, D), lambda j, i: (i, 0)),
                  pl.BlockSpec((None, D, tn), lambda j, i: (l, 0, j)),
                  pl.BlockSpec((None, D, tn), lambda j, i: (l, 0, nj + j)),
                  pl.BlockSpec((FFN_K, tn), lambda j, i: (0, j)),
                  pl.BlockSpec((FFN_K, tn), lambda j, i: (0, nj + j)),
                  pl.BlockSpec((1, tn), lambda j, i: (0, j)),
                  pl.BlockSpec((1, tn), lambda j, i: (0, nj + j))],
        out_specs=pl.BlockSpec((tm, tn), lambda j, i: (i, j)),
        out_shape=jax.ShapeDtypeStruct((T, dff), MXU_DT),
        scratch_shapes=[pltpu.VMEM((FFN_HALO + tm, tn), f32), pltpu.VMEM((FFN_HALO + tm, tn), f32)],
        compiler_params=_cparams(("parallel", "arbitrary")),
        name="ffnup",
    )(h2, ffn_up, ffn_up, ffn_dw, ffn_dw, ffn_db, ffn_db)


def _ffndown_body(g_ref, w_ref, x_ref, o_ref):
    @pl.when(pl.program_id(1) == 0)
    def _():
        o_ref[...] = x_ref[...]
    o_ref[...] += jnp.dot(g_ref[...], w_ref[...], preferred_element_type=f32)


def _ffndown(g, ffn_down, x1, l):
    T, D = x1.shape
    dff = g.shape[1]
    tm, tk = min(1024, T), 512
    return pl.pallas_call(
        _ffndown_body,
        grid=(T // tm, dff // tk),
        in_specs=[pl.BlockSpec((tm, tk), lambda i, k: (i, k)),
                  pl.BlockSpec((None, tk, D), lambda i, k: (l, k, 0)),
                  pl.BlockSpec((tm, D), lambda i, k: (i, 0))],
        out_specs=pl.BlockSpec((tm, D), lambda i, k: (i, 0)),
        out_shape=jax.ShapeDtypeStruct((T, D), f32),
        compiler_params=_cparams(("parallel", "arbitrary")),
        name="ffndown",
    )(g, ffn_down, x1)


def _mixers(z3, p):
    B, S, _ = z3.shape
    qt, kt, vt = _foxprep(z3, p["fox_fb"], p["fox_qg"], p["fox_kg"])
    o_t = _fox_attention(qt, kt, jnp.swapaxes(vt, 2, 3))
    o_a = jnp.transpose(o_t, (0, 3, 1, 2)).reshape(B, S, BRANCH_W)
    o_b = _convmix(z3, p["conv_dw"], p["conv_db"], p["conv_ln_g"], p["conv_ln_b"])
    o_c = _glamix(z3, p["gla_wa"], p["gla_ba"], p["gla_og"])
    o_d = _poolmix(z3, p["pool_w"], p["pool_scale"])
    return o_a, o_b, o_c, o_d


STACKED = ("w_main", "w_gate", "w_branch", "w_out", "ffn_up", "ffn_down")


def _layer(x, params, l):
    B, S, D = x.shape
    T = B * S
    x2 = x.reshape(T, D)
    p = {k: (v if k in STACKED else v[l]) for k, v in params.items()}
    h, z = _inproj(x2, p["norm1_g"], p["w_main"], l)
    outs = _mixers(z.reshape(B, S, Z_COLS), p)
    y = _merge(h, [o.reshape(T, BRANCH_W) for o in outs], p["w_gate"], p["gate_b"], p["w_branch"], l)
    x1, h2 = _outproj(y, x2, p["w_out"], p["norm2_g"], l)
    g = _ffnup(h2, p["ffn_up"], p["ffn_dw"], p["ffn_db"], S, l)
    return _ffndown(g, p["ffn_down"], x1, l).reshape(B, S, D)


def _prepare(norm1_g, w_in, fox_fb, fox_qg, fox_kg, conv_dw, conv_db, conv_ln_g, conv_ln_b,
             gla_wa, gla_ba, gla_og, pool_w, pool_scale, gate_b, w_branch, w_out,
             norm2_g, ffn_up, ffn_dw, ffn_db, ffn_down):
    L, D, _ = w_in.shape
    n_small = 3 * BRANCH_W + FOX_HEADS + 2 * BRANCH_W + 2 * GLA_DK + GLA_DV + GLA_RANK + GLA_DV + BRANCH_W
    o_ff = 3 * BRANCH_W
    o_cz = o_ff + FOX_HEADS
    o_gq = o_cz + 2 * BRANCH_W
    o_ga = o_gq + 2 * GLA_DK + GLA_DV
    o_gr = o_ga + GLA_RANK
    o_pz = o_gr + GLA_DV
    w_main = jnp.concatenate(
        [w_in[:, :, 0:o_ff], w_in[:, :, o_cz:o_ga], w_in[:, :, o_gr:n_small],
         w_in[:, :, o_ff:o_cz], w_in[:, :, o_ga:o_gr],
         jnp.zeros((L, D, Z_COLS - n_small), w_in.dtype)], axis=2).astype(MXU_DT)
    row = lambda a: a[:, None, :]
    pad_lanes = lambda a, off: jnp.pad(a, ((0, 0), (off, LANES - off - a.shape[1])))
    wa_pad = jnp.pad(gla_wa, ((0, 0), (SM_GA, LANES - SM_GA - GLA_RANK), (0, 0)))
    return {
        "norm1_g": row(norm1_g),
        "w_main": w_main,
        "w_gate": w_in[:, :, n_small:].astype(MXU_DT),
        "fox_fb": row(pad_lanes(fox_fb, SM_FF)),
        "fox_qg": row(jnp.tile(fox_qg, (1, FOX_HEADS))),
        "fox_kg": row(jnp.tile(fox_kg, (1, FOX_HEADS))),
        "conv_dw": jnp.pad(conv_dw, ((0, 0), (0, 1), (0, 0))),
        "conv_db": row(conv_db), "conv_ln_g": row(conv_ln_g), "conv_ln_b": row(conv_ln_b),
        "gla_wa": wa_pad.astype(MXU_DT), "gla_ba": row(gla_ba), "gla_og": row(gla_og),
        "pool_w": pool_w.astype(MXU_DT), "pool_scale": row(pool_scale),
        "gate_b": row(gate_b),
        "w_branch": w_branch.astype(MXU_DT),
        "w_out": w_out.astype(MXU_DT),
        "norm2_g": row(norm2_g),
        "ffn_up": ffn_up.astype(MXU_DT), "ffn_dw": ffn_dw, "ffn_db": row(ffn_db),
        "ffn_down": ffn_down.astype(MXU_DT),
    }


def kernel(x, norm1_g, w_in, fox_fb, fox_qg, fox_kg, conv_dw, conv_db, conv_ln_g, conv_ln_b, gla_wa, gla_ba, gla_og, pool_w, pool_scale, gate_b, w_branch, w_out, norm2_g, ffn_up, ffn_dw, ffn_db, ffn_down):
    params = _prepare(norm1_g, w_in, fox_fb, fox_qg, fox_kg, conv_dw, conv_db, conv_ln_g, conv_ln_b,
                      gla_wa, gla_ba, gla_og, pool_w, pool_scale, gate_b, w_branch, w_out,
                      norm2_g, ffn_up, ffn_dw, ffn_db, ffn_down)
    for l in range(w_in.shape[0]):
        x = _layer(x, params, l)
    return x
```

```python
import functools

import jax
import jax.numpy as jnp
from jax import lax
from jax.experimental import pallas as pl
from jax.experimental.pallas import tpu as pltpu

f32 = jnp.float32
MXU_DT = jnp.bfloat16
HI = lax.Precision.HIGHEST

EPS = 1e-6
BRANCH_W = 512
FOX_HEADS = 8
FOX_HD = 64
CONV_K = 31
GLA_HEADS = 4
GLA_DK = 256
GLA_DV = 512
GLA_RANK = 16
GLA_TEMP = 16.0
GLA_CHUNK = 16
POOL_WINDOWS = (2, 4, 8, 16)
FFN_K = 3
LANES = 128

Z_FQ, Z_FK, Z_FV, Z_CA, Z_CG = 0, 512, 1024, 1536, 2048
Z_GQ, Z_GK, Z_GV, Z_GR, Z_PZ, Z_SM, Z_COLS = 2560, 2816, 3072, 3584, 4096, 4608, 5120
SM_FF, SM_GA = 0, 8

NEG_BIG = -1e30
LOG2E = 1.4426950408889634
VMEM_LIMIT = 56 * 1024 * 1024


def _cparams(sem):
    return pltpu.CompilerParams(dimension_semantics=sem, vmem_limit_bytes=VMEM_LIMIT)


def _group(idx, size):
    assert size & (size - 1) == 0
    return idx >> (size.bit_length() - 1)


def _log_sigmoid(x):
    return jnp.minimum(x, 0.0) - jnp.log1p(jnp.exp(-jnp.abs(x)))


def _inproj_body(x_ref, g_ref, w_ref, h_ref, z_ref, hs_ref):
    @pl.when(pl.program_id(1) == 0)
    def _():
        x = x_ref[...]
        ms = jnp.mean(x * x, axis=-1, keepdims=True)
        h = (x * lax.rsqrt(ms + EPS) * g_ref[...]).astype(hs_ref.dtype)
        hs_ref[...] = h
        h_ref[...] = h
    z_ref[...] = jnp.dot(hs_ref[...], w_ref[...], preferred_element_type=f32)


def _inproj(x2, g, w_main, l):
    T, D = x2.shape
    tm, tn = min(1024, T), 1024
    return pl.pallas_call(
        _inproj_body,
        grid=(T // tm, Z_COLS // tn),
        in_specs=[pl.BlockSpec((tm, D), lambda i, j: (i, 0)),
                  pl.BlockSpec((1, D), lambda i, j: (0, 0)),
                  pl.BlockSpec((None, D, tn), lambda i, j: (l, 0, j))],
        out_specs=[pl.BlockSpec((tm, D), lambda i, j: (i, 0)),
                   pl.BlockSpec((tm, tn), lambda i, j: (i, j))],
        out_shape=[jax.ShapeDtypeStruct((T, D), MXU_DT),
                   jax.ShapeDtypeStruct((T, Z_COLS), f32)],
        scratch_shapes=[pltpu.VMEM((tm, D), MXU_DT)],
        compiler_params=_cparams(("parallel", "arbitrary")),
        name="inproj",
    )(x2, g, w_main)


def _head_rmsnorm(x, g, lo):
    sq = x * x
    s0 = jnp.sum(jnp.where(lo, sq, 0.0), axis=-1, keepdims=True)
    s1 = jnp.sum(jnp.where(lo, 0.0, sq), axis=-1, keepdims=True)
    r = jnp.where(lo, lax.rsqrt(s0 * (1.0 / FOX_HD) + EPS), lax.rsqrt(s1 * (1.0 / FOX_HD) + EPS))
    return x * r * g


N_SPLIT = 3
L_KC = FOX_HD
L_QC = FOX_HD + N_SPLIT
L_ONE = FOX_HD


def _split_select(x, lane, first, sign):
    out = jnp.zeros_like(x)
    rest = x * sign
    for j in range(N_SPLIT):
        piece = rest.astype(jnp.bfloat16).astype(f32)
        out = jnp.where(lane == first + j, piece, out)
        rest = rest - piece
    return out


def _foxprep_body(q_ref, k_ref, v_ref, sm_ref, fb_ref, qg_ref, kg_ref,
                  qt_ref, kt_ref, vt_ref, carry_ref, *, tp):
    i = pl.program_id(1)

    @pl.when(i == 0)
    def _():
        carry_ref[...] = jnp.zeros_like(carry_ref)

    logf = _log_sigmoid(sm_ref[...] + fb_ref[...])
    row = lax.broadcasted_iota(jnp.int32, (tp, tp), 0)
    col = lax.broadcasted_iota(jnp.int32, (tp, tp), 1)
    tri = (col <= row).astype(f32)
    c = jnp.dot(tri, logf, preferred_element_type=f32, precision=HI) + carry_ref[0:1, :]
    carry_ref[0:1, :] = c[tp - 1:tp, :]

    lane = lax.broadcasted_iota(jnp.int32, (1, LANES), 1)
    lo = lane < FOX_HD
    ones_q = ((lane >= L_KC) & (lane < L_KC + N_SPLIT)).astype(f32)
    ones_k = ((lane >= L_QC) & (lane < L_QC + N_SPLIT)).astype(f32)
    ones_v = (lane == L_ONE).astype(f32)
    scale = FOX_HD ** -0.5 * LOG2E
    for pair in range(BRANCH_W // LANES):
        sl = slice(pair * LANES, (pair + 1) * LANES)
        qn = _head_rmsnorm(q_ref[:, sl], qg_ref[:, sl], lo) * scale
        kn = _head_rmsnorm(k_ref[:, sl], kg_ref[:, sl], lo)
        vv = v_ref[:, sl]
        for e in (0, 1):
            h = 2 * pair + e
            if e == 1:
                qn, kn, vv = (pltpu.roll(t, FOX_HD, axis=1) for t in (qn, kn, vv))
            cb = jnp.broadcast_to(c[:, h:h + 1], (tp, LANES)) * LOG2E
            qt_ref[h] = jnp.where(lo, qn, _split_select(cb, lane, L_QC, 1.0) + ones_q).astype(qt_ref.dtype)
            kt_ref[h] = jnp.where(lo, kn, _split_select(cb, lane, L_KC, -1.0) + ones_k).astype(kt_ref.dtype)
            vt_ref[h] = jnp.where(lo, vv, ones_v).T.astype(vt_ref.dtype)


def _foxprep(z3, fb_pad, qg, kg):
    B, S, _ = z3.shape
    tp = min(512, S)
    blk = lambda w, c: pl.BlockSpec((None, tp, w), lambda b, i, c=c: (b, i, c))
    vec = lambda w: pl.BlockSpec((1, w), lambda b, i: (0, 0))
    head_blk = pl.BlockSpec((None, FOX_HEADS, tp, LANES), lambda b, i: (b, 0, i, 0))
    return pl.pallas_call(
        functools.partial(_foxprep_body, tp=tp),
        grid=(B, S // tp),
        in_specs=[blk(BRANCH_W, Z_FQ // BRANCH_W), blk(BRANCH_W, Z_FK // BRANCH_W),
                  blk(BRANCH_W, Z_FV // BRANCH_W), blk(LANES, Z_SM // LANES),
                  vec(LANES), vec(BRANCH_W), vec(BRANCH_W)],
        out_specs=[head_blk, head_blk,
                   pl.BlockSpec((None, FOX_HEADS, LANES, tp), lambda b, i: (b, 0, 0, i))],
        out_shape=[jax.ShapeDtypeStruct((B, FOX_HEADS, S, LANES), MXU_DT)] * 2
        + [jax.ShapeDtypeStruct((B, FOX_HEADS, LANES, S), MXU_DT)],
        scratch_shapes=[pltpu.VMEM((8, LANES), f32)],
        compiler_params=_cparams(("parallel", "arbitrary")),
        name="foxprep",
    )(z3, z3, z3, z3, fb_pad, qg, kg)


def _fox_body(q_ref, k_ref, v_ref, o_ref, m_ref, acc_ref, *, tq):
    i = pl.program_id(2)
    m_ref[...] = jnp.full_like(m_ref, NEG_BIG)
    acc_ref[...] = jnp.zeros_like(acc_ref)

    heads = range(FOX_GROUP)

    def blocks(j, masked):
        start = pl.multiple_of(j * tq, tq)
        s = [lax.dot_general(k_ref[g, pl.ds(start, tq), :], q_ref[g], (((1,), (1,)), ((), ())),
                             preferred_element_type=f32) for g in heads]
        if masked:
            key = lax.broadcasted_iota(jnp.int32, (tq, tq), 0)
            qry = lax.broadcasted_iota(jnp.int32, (tq, tq), 1)
            s = [jnp.where(key <= qry, sg, NEG_BIG) for sg in s]
        m_prev = [m_ref[g, 0:1, :] for g in heads]
        m_new = [jnp.maximum(m_prev[g], jnp.max(s[g], axis=0, keepdims=True)) for g in heads]
        p = [jnp.exp2(s[g] - m_new[g]).astype(q_ref.dtype) for g in heads]
        pv = [jnp.dot(v_ref[g, :, pl.ds(start, tq)], p[g], preferred_element_type=f32) for g in heads]
        for g in heads:
            m_ref[g, 0:1, :] = m_new[g]
            acc_ref[g] = jnp.exp2(m_prev[g] - m_new[g]) * acc_ref[g] + pv[g]

    def two_full_blocks(jj, carry):
        blocks(2 * jj, False)
        blocks(2 * jj + 1, False)
        return carry

    lax.fori_loop(0, i // 2, two_full_blocks, 0)

    @pl.when(i % 2 == 1)
    def _():
        blocks(i - 1, False)

    blocks(i, True)
    for pair in range(FOX_GROUP // 2):
        o_t = jnp.concatenate(
            [acc_ref[g, 0:FOX_HD, :] / acc_ref[g, L_ONE:L_ONE + 1, :] for g in (2 * pair, 2 * pair + 1)], axis=0)
        o_ref[:, pair * LANES:(pair + 1) * LANES] = o_t.T.astype(o_ref.dtype)


FOX_GROUP = 8


def _fox_attention(qt, kt, vtt):
    B, H, S, _ = qt.shape
    tq = min(256, S)
    G = FOX_GROUP
    assert G == H
    resident = dict(pipeline_mode=pl.Buffered(1))
    return pl.pallas_call(
        functools.partial(_fox_body, tq=tq),
        grid=(B, H // G, S // tq),
        in_specs=[pl.BlockSpec((None, G, tq, LANES), lambda b, h, i: (b, h, i, 0)),
                  pl.BlockSpec((None, G, S, LANES), lambda b, h, i: (b, h, 0, 0), **resident),
                  pl.BlockSpec((None, G, LANES, S), lambda b, h, i: (b, h, 0, 0), **resident)],
        out_specs=pl.BlockSpec((None, tq, H * FOX_HD), lambda b, h, i: (b, i, 0)),
        out_shape=jax.ShapeDtypeStruct((B, S, H * FOX_HD), MXU_DT),
        scratch_shapes=[pltpu.VMEM((G, 8, tq), f32), pltpu.VMEM((G, LANES, tq), f32)],
        compiler_params=_cparams(("parallel", "parallel", "arbitrary")),
        name="fox_attn",
    )(qt, kt, vtt)


CONV_HALO = 32
CONV_ROWS = 64


def _convmix_body(a_ref, g_ref, dw_ref, db_ref, lng_ref, lnb_ref, o_ref, u_ref, *, tp):
    @pl.when(pl.program_id(1) == 0)
    def _():
        u_ref[0:CONV_HALO, :] = jnp.zeros((CONV_HALO, BRANCH_W), f32)

    u_ref[CONV_HALO:CONV_HALO + tp, :] = a_ref[...] * jax.nn.sigmoid(g_ref[...])
    for r in range(0, tp, CONV_ROWS):
        acc = jnp.broadcast_to(db_ref[...], (CONV_ROWS, BRANCH_W))
        for k in range(CONV_K):
            off = r + CONV_HALO - (CONV_K - 1) + k
            acc = acc + dw_ref[k:k + 1, :] * u_ref[off:off + CONV_ROWS, :]
        mu = jnp.mean(acc, axis=-1, keepdims=True)
        d = acc - mu
        var = jnp.mean(d * d, axis=-1, keepdims=True)
        y = d * lax.rsqrt(var + EPS) * lng_ref[...] + lnb_ref[...]
        o_ref[r:r + CONV_ROWS, :] = (y * jax.nn.sigmoid(y)).astype(o_ref.dtype)
    u_ref[0:CONV_HALO, :] = u_ref[tp:tp + CONV_HALO, :]


def _convmix(z3, dw_pad, db, lng, lnb):
    B, S, _ = z3.shape
    tp = min(256, S)
    blk = lambda c: pl.BlockSpec((None, tp, BRANCH_W), lambda b, i, c=c: (b, i, c))
    vec = pl.BlockSpec((1, BRANCH_W), lambda b, i: (0, 0))
    return pl.pallas_call(
        functools.partial(_convmix_body, tp=tp),
        grid=(B, S // tp),
        in_specs=[blk(Z_CA // BRANCH_W), blk(Z_CG // BRANCH_W),
                  pl.BlockSpec(dw_pad.shape, lambda b, i: (0, 0)), vec, vec, vec],
        out_specs=blk(0),
        out_shape=jax.ShapeDtypeStruct((B, S, BRANCH_W), MXU_DT),
        scratch_shapes=[pltpu.VMEM((CONV_HALO + tp, BRANCH_W), f32)],
        compiler_params=_cparams(("parallel", "arbitrary")),
        name="convmix",
    )(z3, z3, dw_pad, db, lng, lnb)


POOL_HALO = 16


def _poolmix_body(u_in_ref, pw_ref, sc_ref, o_ref, u_ref, *, tp):
    i = pl.program_id(1)

    @pl.when(i == 0)
    def _():
        u_ref[0:POOL_HALO, :] = jnp.zeros((POOL_HALO, BRANCH_W), f32)

    u_ref[POOL_HALO:POOL_HALO + tp, :] = u_in_ref[...]
    pos = i * tp + lax.broadcasted_iota(jnp.int32, (tp, 1), 0)
    for gi, w in enumerate(POOL_WINDOWS):
        sl = slice(gi * LANES, (gi + 1) * LANES)
        acc = u_ref[POOL_HALO:POOL_HALO + tp, sl]
        for j in range(1, w):
            acc = acc + u_ref[POOL_HALO - j:POOL_HALO - j + tp, sl]
        cnt = jnp.minimum(pos + 1, w).astype(f32)
        mixed = acc / cnt - u_ref[POOL_HALO:POOL_HALO + tp, sl]
        out = jnp.dot(mixed.astype(pw_ref.dtype), pw_ref[gi], preferred_element_type=f32)
        o_ref[:, sl] = (out * sc_ref[:, sl]).astype(o_ref.dtype)
    u_ref[0:POOL_HALO, :] = u_ref[tp:tp + POOL_HALO, :]


def _poolmix(z3, pw, scale):
    B, S, _ = z3.shape
    tp = min(512, S)
    return pl.pallas_call(
        functools.partial(_poolmix_body, tp=tp),
        grid=(B, S // tp),
        in_specs=[pl.BlockSpec((None, tp, BRANCH_W), lambda b, i: (b, i, Z_PZ // BRANCH_W)),
                  pl.BlockSpec(pw.shape, lambda b, i: (0, 0, 0)),
                  pl.BlockSpec((1, BRANCH_W), lambda b, i: (0, 0))],
        out_specs=pl.BlockSpec((None, tp, BRANCH_W), lambda b, i: (b, i, 0)),
        out_shape=jax.ShapeDtypeStruct((B, S, BRANCH_W), MXU_DT),
        scratch_shapes=[pltpu.VMEM((POOL_HALO + tp, BRANCH_W), f32)],
        compiler_params=_cparams(("parallel", "arbitrary")),
        name="poolmix",
    )(z3, pw, scale)


def _gla_body(q_ref, k_ref, v_ref, r_ref, sm_ref, wa_ref, ba_ref, og_ref, o_ref,
              st_ref, kb_ref, bb_ref, vb_ref, qlo_ref, qhi_ref, kd_ref, vh_ref, dec_ref, oi_ref, *, tg):
    C = GLA_CHUNK
    hk = GLA_DK // GLA_HEADS
    hv = GLA_DV // GLA_HEADS

    @pl.when(pl.program_id(1) == 0)
    def _():
        st_ref[...] = jnp.zeros_like(st_ref)

    pre = jnp.dot(sm_ref[...].astype(wa_ref.dtype), wa_ref[...], preferred_element_type=f32) + ba_ref[...]
    loga = _log_sigmoid(pre) * (LOG2E / GLA_TEMP)
    row = lax.broadcasted_iota(jnp.int32, (tg, tg), 0)
    col = lax.broadcasted_iota(jnp.int32, (tg, tg), 1)
    same = _group(row, C) == _group(col, C)
    bc = jnp.dot((same & (col <= row)).astype(f32), loga, preferred_element_type=f32, precision=HI)
    blast = jnp.dot(same.astype(f32), loga, preferred_element_type=f32, precision=HI)

    q = q_ref[...] * (hk ** -0.5)
    k = k_ref[...]
    qd = q * jnp.exp2(bc)
    first = (lax.broadcasted_iota(jnp.int32, (1, GLA_DK), 1) & (LANES - 1)) < hk
    qlo_ref[...] = jnp.where(first, qd, 0.0).astype(qlo_ref.dtype)
    qhi_ref[...] = jnp.where(first, 0.0, qd).astype(qhi_ref.dtype)
    kd_ref[...] = (k * jnp.exp2(blast - bc)).astype(kd_ref.dtype)
    dec_ref[...] = jnp.exp2(blast)
    vh_ref[...] = v_ref[...].astype(vh_ref.dtype)

    kb_ref[0:C, :] = jnp.zeros((C, GLA_DK), f32)
    bb_ref[0:C, :] = jnp.zeros((C, GLA_DK), f32)
    vb_ref[0:C, :] = jnp.zeros((C, GLA_DV), f32)
    kb_ref[C:C + tg, :] = k
    bb_ref[C:C + tg, :] = bc
    vb_ref[C:C + tg, :] = v_ref[...]
    rb = (_group(lax.broadcasted_iota(jnp.int32, (GLA_DK, GLA_DV), 0), hk)
          == _group(lax.broadcasted_iota(jnp.int32, (GLA_DK, GLA_DV), 1), hv)).astype(MXU_DT)
    rpos = lax.broadcasted_iota(jnp.int32, (tg, 1), 0) & (C - 1)
    lo = lax.broadcasted_iota(jnp.int32, (1, LANES), 1) < hk
    st = [st_ref[p] for p in range(GLA_HEADS // 2)]

    def intra(delta, acc):
        valid = rpos >= delta
        ks = kb_ref[C - delta:C - delta + tg, :]
        bs = bb_ref[C - delta:C - delta + tg, :]
        vs = vb_ref[C - delta:C - delta + tg, :]
        w = jnp.where(valid, q * ks * jnp.exp2(jnp.where(valid, bc - bs, 0.0)), 0.0)
        return acc + jnp.dot(w.astype(MXU_DT), rb, preferred_element_type=f32) * vs

    def recurrence(n):
        rows = slice(n * C, (n + 1) * C)
        for p in range(GLA_HEADS // 2):
            ps = slice(p * LANES, (p + 1) * LANES)
            lhs = jnp.concatenate([qlo_ref[rows, ps], qhi_ref[rows, ps]], axis=0)
            o_p = lax.dot_general(lhs, st[p].astype(MXU_DT), (((1,), (1,)), ((), ())),
                                  preferred_element_type=f32)
            upd = []
            for e in (0, 1):
                vs = slice((2 * p + e) * hv, (2 * p + e + 1) * hv)
                oi_ref[rows, vs] = o_p[e * C:(e + 1) * C]
                upd.append(lax.dot_general(vh_ref[rows, vs], kd_ref[rows, ps], (((0,), (0,)), ((), ())),
                                           preferred_element_type=f32))
            st[p] = st[p] * dec_ref[n * C:n * C + 1, ps] + jnp.where(lo, upd[0], upd[1])

    n_steps = tg // C
    o_intra = jnp.zeros((tg, GLA_DV), f32)
    for it in range(max(C, n_steps)):
        if it < C:
            o_intra = intra(it, o_intra)
        if it < n_steps:
            recurrence(it)
    for p in range(GLA_HEADS // 2):
        st_ref[p] = st[p]
    oi_ref[...] += o_intra

    for h in range(GLA_HEADS):
        sl = slice(h * hv, (h + 1) * hv)
        oh = oi_ref[:, sl]
        oh = oh * lax.rsqrt(jnp.mean(oh * oh, axis=-1, keepdims=True) + EPS) * og_ref[:, sl]
        rr = r_ref[:, sl]
        o_ref[:, sl] = (oh * (rr * jax.nn.sigmoid(rr))).astype(o_ref.dtype)


def _glamix(z3, wa_pad, ba, og):
    B, S, _ = z3.shape
    tg = min(256, S)
    blk = lambda w, c: pl.BlockSpec((None, tg, w), lambda b, i, c=c: (b, i, c))
    vec = lambda w: pl.BlockSpec((1, w), lambda b, i: (0, 0))
    return pl.pallas_call(
        functools.partial(_gla_body, tg=tg),
        grid=(B, S // tg),
        in_specs=[blk(GLA_DK, Z_GQ // GLA_DK), blk(GLA_DK, Z_GK // GLA_DK),
                  blk(GLA_DV, Z_GV // GLA_DV), blk(GLA_DV, Z_GR // GLA_DV),
                  blk(LANES, Z_SM // LANES),
                  pl.BlockSpec((LANES, GLA_DK), lambda b, i: (0, 0)), vec(GLA_DK), vec(GLA_DV)],
        out_specs=blk(GLA_DV, 0),
        out_shape=jax.ShapeDtypeStruct((B, S, GLA_DV), MXU_DT),
        scratch_shapes=[pltpu.VMEM((GLA_HEADS // 2, GLA_DV // GLA_HEADS, LANES), f32),
                        pltpu.VMEM((GLA_CHUNK + tg, GLA_DK), f32),
                        pltpu.VMEM((GLA_CHUNK + tg, GLA_DK), f32),
                        pltpu.VMEM((GLA_CHUNK + tg, GLA_DV), f32),
                        pltpu.VMEM((tg, GLA_DK), MXU_DT),
                        pltpu.VMEM((tg, GLA_DK), MXU_DT),
                        pltpu.VMEM((tg, GLA_DK), MXU_DT),
                        pltpu.VMEM((tg, GLA_DV), MXU_DT),
                        pltpu.VMEM((tg, GLA_DK), f32),
                        pltpu.VMEM((tg, GLA_DV), f32)],
        compiler_params=_cparams(("parallel", "arbitrary")),
        name="glamix",
    )(z3, z3, z3, z3, z3, wa_pad, ba, og)


def _merge_body(h_ref, oa_ref, ob_ref, oc_ref, od_ref, wg0, wg1, wg2, wg3, gb0, gb1, gb2, gb3,
                p0, p1, p2, p3, y_ref):
    h = h_ref[...]
    acc = None
    for o_ref, wg, gb, p in ((oa_ref, wg0, gb0, p0), (ob_ref, wg1, gb1, p1),
                             (oc_ref, wg2, gb2, p2), (od_ref, wg3, gb3, p3)):
        gate = jax.nn.sigmoid(jnp.dot(h, wg[...], preferred_element_type=f32) + gb[...])
        term = gate * jnp.dot(o_ref[...], p[...], preferred_element_type=f32)
        acc = term if acc is None else acc + term
    y_ref[...] = acc.astype(y_ref.dtype)


def _merge(h, outs, w_gate, gate_b, w_branch, l):
    T, D = h.shape
    tm, tn = min(1024, T), 512
    nj = D // tn
    g0 = Z_COLS // tn
    tok = lambda w: pl.BlockSpec((tm, w), lambda i, j: (i, 0))
    wg = [pl.BlockSpec((None, D, tn), lambda i, j, b=b: (l, 0, g0 + b * nj + j)) for b in range(4)]
    gb = [pl.BlockSpec((1, tn), lambda i, j, b=b: (0, b * nj + j)) for b in range(4)]
    pb = [pl.BlockSpec((None, None, BRANCH_W, tn), lambda i, j, b=b: (l, b, 0, j)) for b in range(4)]
    return pl.pallas_call(
        _merge_body,
        grid=(T // tm, nj),
        in_specs=[tok(D)] + [tok(BRANCH_W)] * 4 + wg + gb + pb,
        out_specs=pl.BlockSpec((tm, tn), lambda i, j: (i, j)),
        out_shape=jax.ShapeDtypeStruct((T, D), MXU_DT),
        compiler_params=_cparams(("parallel", "arbitrary")),
        name="merge",
    )(h, *outs, *([w_gate] * 4), *([gate_b] * 4), *([w_branch] * 4))


def _outproj_body(y_ref, x_ref, w_ref, g_ref, x1_ref, h2_ref):
    x1 = x_ref[...] + jnp.dot(y_ref[...], w_ref[...], preferred_element_type=f32)
    x1_ref[...] = x1
    ms = jnp.mean(x1 * x1, axis=-1, keepdims=True)
    h2_ref[...] = (x1 * lax.rsqrt(ms + EPS) * g_ref[...]).astype(h2_ref.dtype)


def _outproj(y, x2, w_out, g2, l):
    T, D = x2.shape
    tm = min(512, T)
    tok = pl.BlockSpec((tm, D), lambda i: (i, 0))
    return pl.pallas_call(
        _outproj_body,
        grid=(T // tm,),
        in_specs=[tok, tok, pl.BlockSpec((None, D, D), lambda i: (l, 0, 0)),
                  pl.BlockSpec((1, D), lambda i: (0, 0))],
        out_specs=[tok, tok],
        out_shape=[jax.ShapeDtypeStruct((T, D), f32), jax.ShapeDtypeStruct((T, D), MXU_DT)],
        compiler_params=_cparams(("parallel",)),
        name="outproj",
    )(y, x2, w_out, g2)


FFN_HALO = 8


def _ffnup_body(h_ref, wa_ref, wv_ref, dwa_ref, dwv_ref, dba_ref, dbv_ref, g_ref, ua_ref, uv_ref,
                *, tm, tiles_per_seq):
    @pl.when(pl.program_id(1) % tiles_per_seq == 0)
    def _():
        ua_ref[0:FFN_HALO, :] = jnp.zeros((FFN_HALO, ua_ref.shape[1]), f32)
        uv_ref[0:FFN_HALO, :] = jnp.zeros((FFN_HALO, uv_ref.shape[1]), f32)

    h = h_ref[...]

    def conv(w_ref, dw_ref, db_ref, u_ref):
        u_ref[FFN_HALO:FFN_HALO + tm, :] = jnp.dot(h, w_ref[...], preferred_element_type=f32)
        y = db_ref[...] + dw_ref[FFN_K - 1:FFN_K, :] * u_ref[FFN_HALO:FFN_HALO + tm, :]
        for k in range(FFN_K - 1):
            off = FFN_HALO - (FFN_K - 1) + k
            y = y + dw_ref[k:k + 1, :] * u_ref[off:off + tm, :]
        u_ref[0:FFN_HALO, :] = u_ref[tm:tm + FFN_HALO, :]
        return y

    a = conv(wa_ref, dwa_ref, dba_ref, ua_ref)
    v = conv(wv_ref, dwv_ref, dbv_ref, uv_ref)
    g_ref[...] = (a * jax.nn.sigmoid(a) * v).astype(g_ref.dtype)


def _ffnup(h2, ffn_up, ffn_dw, ffn_db, seq_len, l):
    T, D = h2.shape
    dff = ffn_up.shape[2] // 2
    tm, tn = min(1024, seq_len), 512
    nj = dff // tn
    return pl.pallas_call(
        functools.partial(_ffnup_body, tm=tm, tiles_per_seq=seq_len // tm),
        grid=(nj, T // tm),
        in_specs=[pl.BlockSpec((tm, D), lambda j, i: (i, 0)),
                  pl.BlockSpec((None, D, tn), lambda j, i: (l, 0, j)),
                  pl.BlockSpec((None, D, tn), lambda j, i: (l, 0, nj + j)),
                  pl.BlockSpec((FFN_K, tn), lambda j, i: (0, j)),
                  pl.BlockSpec((FFN_K, tn), lambda j, i: (0, nj + j)),
                  pl.BlockSpec((1, tn), lambda j, i: (0, j)),
                  pl.BlockSpec((1, tn), lambda j, i: (0, nj + j))],
        out_specs=pl.BlockSpec((tm, tn), lambda j, i: (i, j)),
        out_shape=jax.ShapeDtypeStruct((T, dff), MXU_DT),
        scratch_shapes=[pltpu.VMEM((FFN_HALO + tm, tn), f32), pltpu.VMEM((FFN_HALO + tm, tn), f32)],
        compiler_params=_cparams(("parallel", "arbitrary")),
        name="ffnup",
    )(h2, ffn_up, ffn_up, ffn_dw, ffn_dw, ffn_db, ffn_db)


def _ffndown_body(g_ref, w_ref, x_ref, o_ref):
    @pl.when(pl.program_id(1) == 0)
    def _():
        o_ref[...] = x_ref[...]
    o_ref[...] += jnp.dot(g_ref[...], w_ref[...], preferred_element_type=f32)


def _ffndown(g, ffn_down, x1, l):
    T, D = x1.shape
    dff = g.shape[1]
    tm, tk = min(1024, T), dff // 4
    return pl.pallas_call(
        _ffndown_body,
        grid=(T // tm, dff // tk),
        in_specs=[pl.BlockSpec((tm, tk), lambda i, k: (i, k)),
                  pl.BlockSpec((None, tk, D), lambda i, k: (l, k, 0)),
                  pl.BlockSpec((tm, D), lambda i, k: (i, 0), pipeline_mode=pl.Buffered(1))],
        out_specs=pl.BlockSpec((tm, D), lambda i, k: (i, 0)),
        out_shape=jax.ShapeDtypeStruct((T, D), f32),
        compiler_params=_cparams(("parallel", "arbitrary")),
        name="ffndown",
    )(g, ffn_down, x1)


def _mixers(z3, p):
    B, S, _ = z3.shape
    qt, kt, vt = _foxprep(z3, p["fox_fb"], p["fox_qg"], p["fox_kg"])
    o_a = _fox_attention(qt, kt, vt)
    o_b = _convmix(z3, p["conv_dw"], p["conv_db"], p["conv_ln_g"], p["conv_ln_b"])
    o_c = _glamix(z3, p["gla_wa"], p["gla_ba"], p["gla_og"])
    o_d = _poolmix(z3, p["pool_w"], p["pool_scale"])
    return o_a, o_b, o_c, o_d


STACKED = ("w_cat", "w_branch", "w_out", "ffn_up", "ffn_down")


def _layer(x, params, l):
    B, S, D = x.shape
    T = B * S
    x2 = x.reshape(T, D)
    p = {k: (v if k in STACKED else v[l]) for k, v in params.items()}
    h, z = _inproj(x2, p["norm1_g"], p["w_cat"], l)
    outs = _mixers(z.reshape(B, S, Z_COLS), p)
    y = _merge(h, [o.reshape(T, BRANCH_W) for o in outs], p["w_cat"], p["gate_b"], p["w_branch"], l)
    x1, h2 = _outproj(y, x2, p["w_out"], p["norm2_g"], l)
    g = _ffnup(h2, p["ffn_up"], p["ffn_dw"], p["ffn_db"], S, l)
    return _ffndown(g, p["ffn_down"], x1, l).reshape(B, S, D)


def _prepare(norm1_g, w_in, fox_fb, fox_qg, fox_kg, conv_dw, conv_db, conv_ln_g, conv_ln_b,
             gla_wa, gla_ba, gla_og, pool_w, pool_scale, gate_b, w_branch, w_out,
             norm2_g, ffn_up, ffn_dw, ffn_db, ffn_down):
    L, D, _ = w_in.shape
    n_small = 3 * BRANCH_W + FOX_HEADS + 2 * BRANCH_W + 2 * GLA_DK + GLA_DV + GLA_RANK + GLA_DV + BRANCH_W
    o_ff = 3 * BRANCH_W
    o_cz = o_ff + FOX_HEADS
    o_gq = o_cz + 2 * BRANCH_W
    o_ga = o_gq + 2 * GLA_DK + GLA_DV
    o_gr = o_ga + GLA_RANK
    o_pz = o_gr + GLA_DV
    w_cat = jnp.concatenate(
        [w_in[:, :, 0:o_ff], w_in[:, :, o_cz:o_ga], w_in[:, :, o_gr:n_small],
         w_in[:, :, o_ff:o_cz], w_in[:, :, o_ga:o_gr],
         jnp.zeros((L, D, Z_COLS - n_small), w_in.dtype), w_in[:, :, n_small:]], axis=2).astype(MXU_DT)
    row = lambda a: a[:, None, :]
    pad_lanes = lambda a, off: jnp.pad(a, ((0, 0), (off, LANES - off - a.shape[1])))
    wa_pad = jnp.pad(gla_wa, ((0, 0), (SM_GA, LANES - SM_GA - GLA_RANK), (0, 0)))
    return {
        "norm1_g": row(norm1_g),
        "w_cat": w_cat,
        "fox_fb": row(pad_lanes(fox_fb, SM_FF)),
        "fox_qg": row(jnp.tile(fox_qg, (1, FOX_HEADS))),
        "fox_kg": row(jnp.tile(fox_kg, (1, FOX_HEADS))),
        "conv_dw": jnp.pad(conv_dw, ((0, 0), (0, 1), (0, 0))),
        "conv_db": row(conv_db), "conv_ln_g": row(conv_ln_g), "conv_ln_b": row(conv_ln_b),
        "gla_wa": wa_pad.astype(MXU_DT), "gla_ba": row(gla_ba), "gla_og": row(gla_og),
        "pool_w": pool_w.astype(MXU_DT), "pool_scale": row(pool_scale),
        "gate_b": row(gate_b),
        "w_branch": w_branch.astype(MXU_DT),
        "w_out": w_out.astype(MXU_DT),
        "norm2_g": row(norm2_g),
        "ffn_up": ffn_up.astype(MXU_DT), "ffn_dw": ffn_dw, "ffn_db": row(ffn_db),
        "ffn_down": ffn_down.astype(MXU_DT),
    }


def kernel(x, norm1_g, w_in, fox_fb, fox_qg, fox_kg, conv_dw, conv_db, conv_ln_g, conv_ln_b, gla_wa, gla_ba, gla_og, pool_w, pool_scale, gate_b, w_branch, w_out, norm2_g, ffn_up, ffn_dw, ffn_db, ffn_down):
    params = _prepare(norm1_g, w_in, fox_fb, fox_qg, fox_kg, conv_dw, conv_db, conv_ln_g, conv_ln_b,
                      gla_wa, gla_ba, gla_og, pool_w, pool_scale, gate_b, w_branch, w_out,
                      norm2_g, ffn_up, ffn_dw, ffn_db, ffn_down)
    for l in range(w_in.shape[0]):
        x = _layer(x, params, l)
    return x
```

```python
import functools

import jax
import jax.numpy as jnp
from jax import lax
from jax.experimental import pallas as pl
from jax.experimental.pallas import tpu as pltpu

f32 = jnp.float32
MXU_DT = jnp.bfloat16
HI = lax.Precision.HIGHEST

EPS = 1e-6
BRANCH_W = 512
FOX_HEADS = 8
FOX_HD = 64
CONV_K = 31
GLA_HEADS = 4
GLA_DK = 256
GLA_DV = 512
GLA_RANK = 16
GLA_TEMP = 16.0
GLA_CHUNK = 16
POOL_WINDOWS = (2, 4, 8, 16)
FFN_K = 3
LANES = 128

Z_FQ, Z_FK, Z_FV, Z_CA, Z_CG = 0, 512, 1024, 1536, 2048
Z_GQ, Z_GK, Z_GV, Z_GR, Z_PZ, Z_SM, Z_COLS = 2560, 2816, 3072, 3584, 4096, 4608, 5120
SM_FF, SM_GA = 0, 8

NEG_BIG = -1e30
LOG2E = 1.4426950408889634
VMEM_LIMIT = 56 * 1024 * 1024


def _cparams(sem):
    return pltpu.CompilerParams(dimension_semantics=sem, vmem_limit_bytes=VMEM_LIMIT)


def _group(idx, size):
    assert size & (size - 1) == 0
    return idx >> (size.bit_length() - 1)


def _log_sigmoid(x):
    return jnp.minimum(x, 0.0) - jnp.log1p(jnp.exp(-jnp.abs(x)))


def _inproj_body(x_ref, g_ref, w_ref, h_ref, z_ref, hs_ref):
    @pl.when(pl.program_id(1) == 0)
    def _():
        x = x_ref[...]
        ms = jnp.mean(x * x, axis=-1, keepdims=True)
        h = (x * lax.rsqrt(ms + EPS) * g_ref[...]).astype(hs_ref.dtype)
        hs_ref[...] = h
        h_ref[...] = h
    z_ref[...] = jnp.dot(hs_ref[...], w_ref[...], preferred_element_type=f32)


def _inproj(x2, g, w_main, l):
    T, D = x2.shape
    tm, tn = min(1024, T), 1024
    return pl.pallas_call(
        _inproj_body,
        grid=(T // tm, Z_COLS // tn),
        in_specs=[pl.BlockSpec((tm, D), lambda i, j: (i, 0)),
                  pl.BlockSpec((1, D), lambda i, j: (0, 0)),
                  pl.BlockSpec((None, D, tn), lambda i, j: (l, 0, j))],
        out_specs=[pl.BlockSpec((tm, D), lambda i, j: (i, 0)),
                   pl.BlockSpec((tm, tn), lambda i, j: (i, j))],
        out_shape=[jax.ShapeDtypeStruct((T, D), MXU_DT),
                   jax.ShapeDtypeStruct((T, Z_COLS), f32)],
        scratch_shapes=[pltpu.VMEM((tm, D), MXU_DT)],
        compiler_params=_cparams(("parallel", "arbitrary")),
        name="inproj",
    )(x2, g, w_main)


def _head_rmsnorm(x, g, lo):
    sq = x * x
    s0 = jnp.sum(jnp.where(lo, sq, 0.0), axis=-1, keepdims=True)
    s1 = jnp.sum(jnp.where(lo, 0.0, sq), axis=-1, keepdims=True)
    r = jnp.where(lo, lax.rsqrt(s0 * (1.0 / FOX_HD) + EPS), lax.rsqrt(s1 * (1.0 / FOX_HD) + EPS))
    return x * r * g


N_SPLIT = 3
L_KC = FOX_HD
L_QC = FOX_HD + N_SPLIT
L_ONE = FOX_HD


def _split_pieces(x):
    pieces, rest = [], x
    for _ in range(N_SPLIT):
        piece = rest.astype(jnp.bfloat16)
        pieces.append(piece)
        rest = rest - piece.astype(f32)
    return pieces


def _foxprep_body(q_ref, k_ref, v_ref, sm_ref, fb_ref, qg_ref, kg_ref,
                  qt_ref, kt_ref, vt_ref, carry_ref, *, tp):
    i = pl.program_id(1)

    @pl.when(i == 0)
    def _():
        carry_ref[...] = jnp.zeros_like(carry_ref)

    logf = _log_sigmoid(sm_ref[...] + fb_ref[...])
    row = lax.broadcasted_iota(jnp.int32, (tp, tp), 0)
    col = lax.broadcasted_iota(jnp.int32, (tp, tp), 1)
    tri = (col <= row).astype(f32)
    c = jnp.dot(tri, logf, preferred_element_type=f32, precision=HI) + carry_ref[0:1, :]
    carry_ref[0:1, :] = c[tp - 1:tp, :]

    lane = lax.broadcasted_iota(jnp.int32, (1, LANES), 1)
    lo = lane < FOX_HD
    ones_q = ((lane >= L_KC) & (lane < L_KC + N_SPLIT)).astype(f32)
    ones_k = ((lane >= L_QC) & (lane < L_QC + N_SPLIT)).astype(f32)
    ones_v = (lane == L_ONE).astype(f32)
    in_qc = (lane >= L_QC) & (lane < L_QC + N_SPLIT)
    in_kc = (lane >= L_KC) & (lane < L_KC + N_SPLIT)
    pieces = [p.astype(f32) for p in _split_pieces(c * LOG2E)]
    scale = FOX_HD ** -0.5 * LOG2E
    for pair in range(BRANCH_W // LANES):
        sl = slice(pair * LANES, (pair + 1) * LANES)
        qn = _head_rmsnorm(q_ref[:, sl], qg_ref[:, sl], lo) * scale
        kn = _head_rmsnorm(k_ref[:, sl], kg_ref[:, sl], lo)
        vv = v_ref[:, sl]
        for e in (0, 1):
            h = 2 * pair + e
            if e == 1:
                qn, kn, vv = (pltpu.roll(t, FOX_HD, axis=1) for t in (qn, kn, vv))
            cp = jnp.zeros((tp, LANES), f32)
            for j, piece in enumerate(pieces):
                pj = jnp.broadcast_to(piece[:, h:h + 1], (tp, LANES))
                cp = jnp.where((lane == L_KC + j) | (lane == L_QC + j), pj, cp)
            qt_ref[h] = jnp.where(lo, qn, jnp.where(in_qc, cp, ones_q)).astype(qt_ref.dtype)
            kt_ref[h] = jnp.where(lo, kn, jnp.where(in_kc, -cp, ones_k)).astype(kt_ref.dtype)
            vt_ref[h] = jnp.where(lo, vv, ones_v).T.astype(vt_ref.dtype)


def _foxprep(z3, fb_pad, qg, kg):
    B, S, _ = z3.shape
    tp = min(512, S)
    blk = lambda w, c: pl.BlockSpec((None, tp, w), lambda b, i, c=c: (b, i, c))
    vec = lambda w: pl.BlockSpec((1, w), lambda b, i: (0, 0))
    head_blk = pl.BlockSpec((None, FOX_HEADS, tp, LANES), lambda b, i: (b, 0, i, 0))
    return pl.pallas_call(
        functools.partial(_foxprep_body, tp=tp),
        grid=(B, S // tp),
        in_specs=[blk(BRANCH_W, Z_FQ // BRANCH_W), blk(BRANCH_W, Z_FK // BRANCH_W),
                  blk(BRANCH_W, Z_FV // BRANCH_W), blk(LANES, Z_SM // LANES),
                  vec(LANES), vec(BRANCH_W), vec(BRANCH_W)],
        out_specs=[head_blk, head_blk,
                   pl.BlockSpec((None, FOX_HEADS, LANES, tp), lambda b, i: (b, 0, 0, i))],
        out_shape=[jax.ShapeDtypeStruct((B, FOX_HEADS, S, LANES), MXU_DT)] * 2
        + [jax.ShapeDtypeStruct((B, FOX_HEADS, LANES, S), MXU_DT)],
        scratch_shapes=[pltpu.VMEM((8, LANES), f32)],
        compiler_params=_cparams(("parallel", "arbitrary")),
        name="foxprep",
    )(z3, z3, z3, z3, fb_pad, qg, kg)


def _fox_body(q_ref, k_ref, v_ref, o_ref, m_ref, acc_ref, *, tq):
    i = pl.program_id(2)
    m_ref[...] = jnp.full_like(m_ref, NEG_BIG)
    acc_ref[...] = jnp.zeros_like(acc_ref)

    heads = range(FOX_GROUP)

    def blocks(j, masked):
        start = pl.multiple_of(j * tq, tq)
        s = [lax.dot_general(k_ref[g, pl.ds(start, tq), :], q_ref[g], (((1,), (1,)), ((), ())),
                             preferred_element_type=f32) for g in heads]
        if masked:
            key = lax.broadcasted_iota(jnp.int32, (tq, tq), 0)
            qry = lax.broadcasted_iota(jnp.int32, (tq, tq), 1)
            s = [jnp.where(key <= qry, sg, NEG_BIG) for sg in s]
        m_prev = [m_ref[g, 0:1, :] for g in heads]
        m_new = [jnp.maximum(m_prev[g], jnp.max(s[g], axis=0, keepdims=True)) for g in heads]
        p = [jnp.exp2(s[g] - m_new[g]).astype(q_ref.dtype) for g in heads]
        pv = [jnp.dot(v_ref[g, :, pl.ds(start, tq)], p[g], preferred_element_type=f32) for g in heads]
        for g in heads:
            m_ref[g, 0:1, :] = m_new[g]
            acc_ref[g] = jnp.exp2(m_prev[g] - m_new[g]) * acc_ref[g] + pv[g]

    def two_full_blocks(jj, carry):
        blocks(2 * jj, False)
        blocks(2 * jj + 1, False)
        return carry

    lax.fori_loop(0, i // 2, two_full_blocks, 0)

    @pl.when(i % 2 == 1)
    def _():
        blocks(i - 1, False)

    blocks(i, True)
    for pair in range(FOX_GROUP // 2):
        o_t = jnp.concatenate(
            [acc_ref[g, 0:FOX_HD, :] / acc_ref[g, L_ONE:L_ONE + 1, :] for g in (2 * pair, 2 * pair + 1)], axis=0)
        o_ref[:, pair * LANES:(pair + 1) * LANES] = o_t.T.astype(o_ref.dtype)


FOX_GROUP = 8


def _fox_attention(qt, kt, vtt):
    B, H, S, _ = qt.shape
    tq = min(256, S)
    G = FOX_GROUP
    assert G == H
    resident = dict(pipeline_mode=pl.Buffered(1))
    return pl.pallas_call(
        functools.partial(_fox_body, tq=tq),
        grid=(B, H // G, S // tq),
        in_specs=[pl.BlockSpec((None, G, tq, LANES), lambda b, h, i: (b, h, i, 0)),
                  pl.BlockSpec((None, G, S, LANES), lambda b, h, i: (b, h, 0, 0), **resident),
                  pl.BlockSpec((None, G, LANES, S), lambda b, h, i: (b, h, 0, 0), **resident)],
        out_specs=pl.BlockSpec((None, tq, H * FOX_HD), lambda b, h, i: (b, i, 0)),
        out_shape=jax.ShapeDtypeStruct((B, S, H * FOX_HD), MXU_DT),
        scratch_shapes=[pltpu.VMEM((G, 8, tq), f32), pltpu.VMEM((G, LANES, tq), f32)],
        compiler_params=_cparams(("parallel", "parallel", "arbitrary")),
        name="fox_attn",
    )(qt, kt, vtt)


CONV_HALO = 32
CONV_ROWS = 64


def _convmix_body(a_ref, g_ref, dw_ref, db_ref, lng_ref, lnb_ref, o_ref, u_ref, us_ref, *, tp):
    @pl.when(pl.program_id(1) == 0)
    def _():
        u_ref[0:CONV_HALO, :] = jnp.zeros((CONV_HALO, BRANCH_W), f32)

    u_ref[CONV_HALO:CONV_HALO + tp, :] = a_ref[...] * jax.nn.sigmoid(g_ref[...])
    n_shift = CONV_HALO + tp - 8
    for rho in range(1, 8):
        us_ref[rho - 1, 0:n_shift, :] = u_ref[rho:rho + n_shift, :]
    for r in range(0, tp, CONV_ROWS):
        acc = jnp.broadcast_to(db_ref[...], (CONV_ROWS, BRANCH_W))
        for k in range(CONV_K):
            q8, rho = divmod(CONV_HALO - (CONV_K - 1) + k, 8)
            off = r + 8 * q8
            tap = u_ref[off:off + CONV_ROWS, :] if rho == 0 else us_ref[rho - 1, off:off + CONV_ROWS, :]
            acc = acc + dw_ref[k:k + 1, :] * tap
        mu = jnp.mean(acc, axis=-1, keepdims=True)
        d = acc - mu
        var = jnp.mean(d * d, axis=-1, keepdims=True)
        y = d * lax.rsqrt(var + EPS) * lng_ref[...] + lnb_ref[...]
        o_ref[r:r + CONV_ROWS, :] = (y * jax.nn.sigmoid(y)).astype(o_ref.dtype)
    u_ref[0:CONV_HALO, :] = u_ref[tp:tp + CONV_HALO, :]


def _convmix(z3, dw_pad, db, lng, lnb):
    B, S, _ = z3.shape
    tp = min(256, S)
    blk = lambda c: pl.BlockSpec((None, tp, BRANCH_W), lambda b, i, c=c: (b, i, c))
    vec = pl.BlockSpec((1, BRANCH_W), lambda b, i: (0, 0))
    return pl.pallas_call(
        functools.partial(_convmix_body, tp=tp),
        grid=(B, S // tp),
        in_specs=[blk(Z_CA // BRANCH_W), blk(Z_CG // BRANCH_W),
                  pl.BlockSpec(dw_pad.shape, lambda b, i: (0, 0)), vec, vec, vec],
        out_specs=blk(0),
        out_shape=jax.ShapeDtypeStruct((B, S, BRANCH_W), MXU_DT),
        scratch_shapes=[pltpu.VMEM((CONV_HALO + tp, BRANCH_W), f32),
                        pltpu.VMEM((7, CONV_HALO + tp, BRANCH_W), f32)],
        compiler_params=_cparams(("parallel", "arbitrary")),
        name="convmix",
    )(z3, z3, dw_pad, db, lng, lnb)


POOL_HALO = 16


def _poolmix_body(u_in_ref, pw_ref, sc_ref, o_ref, u_ref, *, tp):
    i = pl.program_id(1)

    @pl.when(i == 0)
    def _():
        u_ref[0:POOL_HALO, :] = jnp.zeros((POOL_HALO, BRANCH_W), f32)

    u_ref[POOL_HALO:POOL_HALO + tp, :] = u_in_ref[...]
    pos = i * tp + lax.broadcasted_iota(jnp.int32, (tp, 1), 0)
    for gi, w in enumerate(POOL_WINDOWS):
        sl = slice(gi * LANES, (gi + 1) * LANES)
        acc = u_ref[POOL_HALO:POOL_HALO + tp, sl]
        for j in range(1, w):
            acc = acc + u_ref[POOL_HALO - j:POOL_HALO - j + tp, sl]
        cnt = jnp.minimum(pos + 1, w).astype(f32)
        mixed = acc / cnt - u_ref[POOL_HALO:POOL_HALO + tp, sl]
        out = jnp.dot(mixed.astype(pw_ref.dtype), pw_ref[gi], preferred_element_type=f32)
        o_ref[:, sl] = (out * sc_ref[:, sl]).astype(o_ref.dtype)
    u_ref[0:POOL_HALO, :] = u_ref[tp:tp + POOL_HALO, :]


def _poolmix(z3, pw, scale):
    B, S, _ = z3.shape
    tp = min(512, S)
    return pl.pallas_call(
        functools.partial(_poolmix_body, tp=tp),
        grid=(B, S // tp),
        in_specs=[pl.BlockSpec((None, tp, BRANCH_W), lambda b, i: (b, i, Z_PZ // BRANCH_W)),
                  pl.BlockSpec(pw.shape, lambda b, i: (0, 0, 0)),
                  pl.BlockSpec((1, BRANCH_W), lambda b, i: (0, 0))],
        out_specs=pl.BlockSpec((None, tp, BRANCH_W), lambda b, i: (b, i, 0)),
        out_shape=jax.ShapeDtypeStruct((B, S, BRANCH_W), MXU_DT),
        scratch_shapes=[pltpu.VMEM((POOL_HALO + tp, BRANCH_W), f32)],
        compiler_params=_cparams(("parallel", "arbitrary")),
        name="poolmix",
    )(z3, pw, scale)


def _gla_body(q_ref, k_ref, v_ref, r_ref, sm_ref, wa_ref, ba_ref, og_ref, o_ref,
              st_ref, kb_ref, bb_ref, vb_ref, qlo_ref, qhi_ref, kd_ref, vh_ref, dec_ref, oi_ref, *, tg):
    C = GLA_CHUNK
    hk = GLA_DK // GLA_HEADS
    hv = GLA_DV // GLA_HEADS

    @pl.when(pl.program_id(1) == 0)
    def _():
        st_ref[...] = jnp.zeros_like(st_ref)

    pre = jnp.dot(sm_ref[...].astype(wa_ref.dtype), wa_ref[...], preferred_element_type=f32) + ba_ref[...]
    loga = _log_sigmoid(pre) * (LOG2E / GLA_TEMP)
    row = lax.broadcasted_iota(jnp.int32, (tg, tg), 0)
    col = lax.broadcasted_iota(jnp.int32, (tg, tg), 1)
    same = _group(row, C) == _group(col, C)
    bc = jnp.dot((same & (col <= row)).astype(f32), loga, preferred_element_type=f32, precision=HI)
    blast = jnp.dot(same.astype(f32), loga, preferred_element_type=f32, precision=HI)

    q = q_ref[...] * (hk ** -0.5)
    k = k_ref[...]
    qd = q * jnp.exp2(bc)
    first = (lax.broadcasted_iota(jnp.int32, (1, GLA_DK), 1) & (LANES - 1)) < hk
    qlo_ref[...] = jnp.where(first, qd, 0.0).astype(qlo_ref.dtype)
    qhi_ref[...] = jnp.where(first, 0.0, qd).astype(qhi_ref.dtype)
    kd_ref[...] = (k * jnp.exp2(blast - bc)).astype(kd_ref.dtype)
    dec_ref[...] = jnp.exp2(blast)
    vh_ref[...] = v_ref[...].astype(vh_ref.dtype)

    kb_ref[0:C, :] = jnp.zeros((C, GLA_DK), f32)
    bb_ref[0:C, :] = jnp.zeros((C, GLA_DK), f32)
    vb_ref[0:C, :] = jnp.zeros((C, GLA_DV), f32)
    kb_ref[C:C + tg, :] = k
    bb_ref[C:C + tg, :] = bc
    vb_ref[C:C + tg, :] = v_ref[...]
    rb = (_group(lax.broadcasted_iota(jnp.int32, (GLA_DK, GLA_DV), 0), hk)
          == _group(lax.broadcasted_iota(jnp.int32, (GLA_DK, GLA_DV), 1), hv)).astype(MXU_DT)
    rpos = lax.broadcasted_iota(jnp.int32, (tg, 1), 0) & (C - 1)
    lo = lax.broadcasted_iota(jnp.int32, (1, LANES), 1) < hk
    st = [st_ref[p] for p in range(GLA_HEADS // 2)]

    def intra(delta, acc):
        valid = rpos >= delta
        ks = kb_ref[C - delta:C - delta + tg, :]
        bs = bb_ref[C - delta:C - delta + tg, :]
        vs = vb_ref[C - delta:C - delta + tg, :]
        w = jnp.where(valid, q * ks * jnp.exp2(jnp.where(valid, bc - bs, 0.0)), 0.0)
        return acc + jnp.dot(w.astype(MXU_DT), rb, preferred_element_type=f32) * vs

    def recurrence(n):
        rows = slice(n * C, (n + 1) * C)
        for p in range(GLA_HEADS // 2):
            ps = slice(p * LANES, (p + 1) * LANES)
            lhs = jnp.concatenate([qlo_ref[rows, ps], qhi_ref[rows, ps]], axis=0)
            o_p = lax.dot_general(lhs, st[p].astype(MXU_DT), (((1,), (1,)), ((), ())),
                                  preferred_element_type=f32)
            upd = []
            for e in (0, 1):
                vs = slice((2 * p + e) * hv, (2 * p + e + 1) * hv)
                oi_ref[rows, vs] = o_p[e * C:(e + 1) * C]
                upd.append(lax.dot_general(vh_ref[rows, vs], kd_ref[rows, ps], (((0,), (0,)), ((), ())),
                                           preferred_element_type=f32))
            st[p] = st[p] * dec_ref[n * C:n * C + 1, ps] + jnp.where(lo, upd[0], upd[1])

    n_steps = tg // C
    o_intra = jnp.zeros((tg, GLA_DV), f32)
    for it in range(max(C, n_steps)):
        if it < C:
            o_intra = intra(it, o_intra)
        if it < n_steps:
            recurrence(it)
    for p in range(GLA_HEADS // 2):
        st_ref[p] = st[p]
    oi_ref[...] += o_intra

    for h in range(GLA_HEADS):
        sl = slice(h * hv, (h + 1) * hv)
        oh = oi_ref[:, sl]
        oh = oh * lax.rsqrt(jnp.mean(oh * oh, axis=-1, keepdims=True) + EPS) * og_ref[:, sl]
        rr = r_ref[:, sl]
        o_ref[:, sl] = (oh * (rr * jax.nn.sigmoid(rr))).astype(o_ref.dtype)


def _glamix(z3, wa_pad, ba, og):
    B, S, _ = z3.shape
    tg = min(256, S)
    blk = lambda w, c: pl.BlockSpec((None, tg, w), lambda b, i, c=c: (b, i, c))
    vec = lambda w: pl.BlockSpec((1, w), lambda b, i: (0, 0))
    return pl.pallas_call(
        functools.partial(_gla_body, tg=tg),
        grid=(B, S // tg),
        in_specs=[blk(GLA_DK, Z_GQ // GLA_DK), blk(GLA_DK, Z_GK // GLA_DK),
                  blk(GLA_DV, Z_GV // GLA_DV), blk(GLA_DV, Z_GR // GLA_DV),
                  blk(LANES, Z_SM // LANES),
                  pl.BlockSpec((LANES, GLA_DK), lambda b, i: (0, 0)), vec(GLA_DK), vec(GLA_DV)],
        out_specs=blk(GLA_DV, 0),
        out_shape=jax.ShapeDtypeStruct((B, S, GLA_DV), MXU_DT),
        scratch_shapes=[pltpu.VMEM((GLA_HEADS // 2, GLA_DV // GLA_HEADS, LANES), f32),
                        pltpu.VMEM((GLA_CHUNK + tg, GLA_DK), f32),
                        pltpu.VMEM((GLA_CHUNK + tg, GLA_DK), f32),
                        pltpu.VMEM((GLA_CHUNK + tg, GLA_DV), f32),
                        pltpu.VMEM((tg, GLA_DK), MXU_DT),
                        pltpu.VMEM((tg, GLA_DK), MXU_DT),
                        pltpu.VMEM((tg, GLA_DK), MXU_DT),
                        pltpu.VMEM((tg, GLA_DV), MXU_DT),
                        pltpu.VMEM((tg, GLA_DK), f32),
                        pltpu.VMEM((tg, GLA_DV), f32)],
        compiler_params=_cparams(("parallel", "arbitrary")),
        name="glamix",
    )(z3, z3, z3, z3, z3, wa_pad, ba, og)


def _merge_body(h_ref, oa_ref, ob_ref, oc_ref, od_ref, wg0, wg1, wg2, wg3, gb0, gb1, gb2, gb3,
                p0, p1, p2, p3, y_ref):
    h = h_ref[...]
    acc = None
    for o_ref, wg, gb, p in ((oa_ref, wg0, gb0, p0), (ob_ref, wg1, gb1, p1),
                             (oc_ref, wg2, gb2, p2), (od_ref, wg3, gb3, p3)):
        gate = jax.nn.sigmoid(jnp.dot(h, wg[...], preferred_element_type=f32) + gb[...])
        term = gate * jnp.dot(o_ref[...], p[...], preferred_element_type=f32)
        acc = term if acc is None else acc + term
    y_ref[...] = acc.astype(y_ref.dtype)


def _merge(h, outs, w_gate, gate_b, w_branch, l):
    T, D = h.shape
    tm, tn = min(1024, T), 512
    nj = D // tn
    g0 = Z_COLS // tn
    tok = lambda w: pl.BlockSpec((tm, w), lambda i, j: (i, 0))
    wg = [pl.BlockSpec((None, D, tn), lambda i, j, b=b: (l, 0, g0 + b * nj + j)) for b in range(4)]
    gb = [pl.BlockSpec((1, tn), lambda i, j, b=b: (0, b * nj + j)) for b in range(4)]
    pb = [pl.BlockSpec((None, None, BRANCH_W, tn), lambda i, j, b=b: (l, b, 0, j)) for b in range(4)]
    return pl.pallas_call(
        _merge_body,
        grid=(T // tm, nj),
        in_specs=[tok(D)] + [tok(BRANCH_W)] * 4 + wg + gb + pb,
        out_specs=pl.BlockSpec((tm, tn), lambda i, j: (i, j)),
        out_shape=jax.ShapeDtypeStruct((T, D), MXU_DT),
        compiler_params=_cparams(("parallel", "arbitrary")),
        name="merge",
    )(h, *outs, *([w_gate] * 4), *([gate_b] * 4), *([w_branch] * 4))


def _outproj_body(y_ref, x_ref, w_ref, g_ref, x1_ref, h2_ref):
    x1 = x_ref[...] + jnp.dot(y_ref[...], w_ref[...], preferred_element_type=f32)
    x1_ref[...] = x1
    ms = jnp.mean(x1 * x1, axis=-1, keepdims=True)
    h2_ref[...] = (x1 * lax.rsqrt(ms + EPS) * g_ref[...]).astype(h2_ref.dtype)


def _outproj(y, x2, w_out, g2, l):
    T, D = x2.shape
    tm = min(512, T)
    tok = pl.BlockSpec((tm, D), lambda i: (i, 0))
    return pl.pallas_call(
        _outproj_body,
        grid=(T // tm,),
        in_specs=[tok, tok, pl.BlockSpec((None, D, D), lambda i: (l, 0, 0)),
                  pl.BlockSpec((1, D), lambda i: (0, 0))],
        out_specs=[tok, tok],
        out_shape=[jax.ShapeDtypeStruct((T, D), f32), jax.ShapeDtypeStruct((T, D), MXU_DT)],
        compiler_params=_cparams(("parallel",)),
        name="outproj",
    )(y, x2, w_out, g2)


FFN_HALO = 8


def _ffnup_body(h_ref, wa_ref, wv_ref, dwa_ref, dwv_ref, dba_ref, dbv_ref, g_ref, ua_ref, uv_ref,
                *, tm, tiles_per_seq):
    @pl.when(pl.program_id(1) % tiles_per_seq == 0)
    def _():
        ua_ref[0:FFN_HALO, :] = jnp.zeros((FFN_HALO, ua_ref.shape[1]), f32)
        uv_ref[0:FFN_HALO, :] = jnp.zeros((FFN_HALO, uv_ref.shape[1]), f32)

    h = h_ref[...]

    def conv(w_ref, dw_ref, db_ref, u_ref):
        u_ref[FFN_HALO:FFN_HALO + tm, :] = jnp.dot(h, w_ref[...], preferred_element_type=f32)
        y = db_ref[...] + dw_ref[FFN_K - 1:FFN_K, :] * u_ref[FFN_HALO:FFN_HALO + tm, :]
        for k in range(FFN_K - 1):
            off = FFN_HALO - (FFN_K - 1) + k
            y = y + dw_ref[k:k + 1, :] * u_ref[off:off + tm, :]
        u_ref[0:FFN_HALO, :] = u_ref[tm:tm + FFN_HALO, :]
        return y

    a = conv(wa_ref, dwa_ref, dba_ref, ua_ref)
    v = conv(wv_ref, dwv_ref, dbv_ref, uv_ref)
    g_ref[...] = (a * jax.nn.sigmoid(a) * v).astype(g_ref.dtype)


def _ffnup(h2, ffn_up, ffn_dw, ffn_db, seq_len, l):
    T, D = h2.shape
    dff = ffn_up.shape[2] // 2
    tm, tn = min(1024, seq_len), 512
    nj = dff // tn
    return pl.pallas_call(
        functools.partial(_ffnup_body, tm=tm, tiles_per_seq=seq_len // tm),
        grid=(nj, T // tm),
        in_specs=[pl.BlockSpec((tm, D), lambda j, i: (i, 0)),
                  pl.BlockSpec((None, D, tn), lambda j, i: (l, 0, j)),
                  pl.BlockSpec((None, D, tn), lambda j, i: (l, 0, nj + j)),
                  pl.BlockSpec((FFN_K, tn), lambda j, i: (0, j)),
                  pl.BlockSpec((FFN_K, tn), lambda j, i: (0, nj + j)),
                  pl.BlockSpec((1, tn), lambda j, i: (0, j)),
                  pl.BlockSpec((1, tn), lambda j, i: (0, nj + j))],
        out_specs=pl.BlockSpec((tm, tn), lambda j, i: (i, j)),
        out_shape=jax.ShapeDtypeStruct((T, dff), MXU_DT),
        scratch_shapes=[pltpu.VMEM((FFN_HALO + tm, tn), f32), pltpu.VMEM((FFN_HALO + tm, tn), f32)],
        compiler_params=_cparams(("parallel", "arbitrary")),
        name="ffnup",
    )(h2, ffn_up, ffn_up, ffn_dw, ffn_dw, ffn_db, ffn_db)


def _ffndown_body(g_ref, w_ref, x_ref, o_ref):
    @pl.when(pl.program_id(1) == 0)
    def _():
        o_ref[...] = x_ref[...]
    o_ref[...] += jnp.dot(g_ref[...], w_ref[...], preferred_element_type=f32)


def _ffndown(g, ffn_down, x1, l):
    T, D = x1.shape
    dff = g.shape[1]
    tm, tk = min(1024, T), 512
    return pl.pallas_call(
        _ffndown_body,
        grid=(T // tm, dff // tk),
        in_specs=[pl.BlockSpec((tm, tk), lambda i, k: (i, k)),
                  pl.BlockSpec((None, tk, D), lambda i, k: (l, k, 0)),
                  pl.BlockSpec((tm, D), lambda i, k: (i, 0))],
        out_specs=pl.BlockSpec((tm, D), lambda i, k: (i, 0)),
        out_shape=jax.ShapeDtypeStruct((T, D), f32),
        compiler_params=_cparams(("parallel", "arbitrary")),
        name="ffndown",
    )(g, ffn_down, x1)


def _mixers(z3, p):
    B, S, _ = z3.shape
    qt, kt, vt = _foxprep(z3, p["fox_fb"], p["fox_qg"], p["fox_kg"])
    o_a = _fox_attention(qt, kt, vt)
    o_b = _convmix(z3, p["conv_dw"], p["conv_db"], p["conv_ln_g"], p["conv_ln_b"])
    o_c = _glamix(z3, p["gla_wa"], p["gla_ba"], p["gla_og"])
    o_d = _poolmix(z3, p["pool_w"], p["pool_scale"])
    return o_a, o_b, o_c, o_d


STACKED = ("w_cat", "w_branch", "w_out", "ffn_up", "ffn_down")


def _layer(x, params, l):
    B, S, D = x.shape
    T = B * S
    x2 = x.reshape(T, D)
    p = {k: (v if k in STACKED else v[l]) for k, v in params.items()}
    h, z = _inproj(x2, p["norm1_g"], p["w_cat"], l)
    outs = _mixers(z.reshape(B, S, Z_COLS), p)
    y = _merge(h, [o.reshape(T, BRANCH_W) for o in outs], p["w_cat"], p["gate_b"], p["w_branch"], l)
    x1, h2 = _outproj(y, x2, p["w_out"], p["norm2_g"], l)
    g = _ffnup(h2, p["ffn_up"], p["ffn_dw"], p["ffn_db"], S, l)
    return _ffndown(g, p["ffn_down"], x1, l).reshape(B, S, D)


def _w_in_pieces(d_model):
    n_small = 3 * BRANCH_W + FOX_HEADS + 2 * BRANCH_W + 2 * GLA_DK + GLA_DV + GLA_RANK + GLA_DV + BRANCH_W
    o_ff = 3 * BRANCH_W
    o_cz = o_ff + FOX_HEADS
    o_ga = o_cz + 2 * BRANCH_W + 2 * GLA_DK + GLA_DV
    o_gr = o_ga + GLA_RANK
    return n_small, [(0, o_ff, Z_FQ), (o_cz, o_ga, Z_CA), (o_gr, n_small, Z_GR),
                     (o_ff, o_cz, Z_SM + SM_FF), (o_ga, o_gr, Z_SM + SM_GA),
                     (n_small, n_small + 4 * d_model, Z_COLS)]


def _relayout_body(w_ref, o_ref, *, pieces, pad):
    for src0, src1, dst in pieces:
        o_ref[:, dst:dst + (src1 - src0)] = w_ref[:, src0:src1].astype(o_ref.dtype)
    o_ref[:, pad[0]:pad[1]] = jnp.zeros((o_ref.shape[0], pad[1] - pad[0]), o_ref.dtype)


def _relayout_w_in(w_in):
    L, D, n_in = w_in.shape
    n_small, pieces = _w_in_pieces(D)
    n_out = Z_COLS + 4 * D
    td = 256
    return pl.pallas_call(
        functools.partial(_relayout_body, pieces=pieces, pad=(Z_SM + SM_GA + GLA_RANK, Z_COLS)),
        grid=(L, D // td),
        in_specs=[pl.BlockSpec((None, td, n_in), lambda l, i: (l, i, 0))],
        out_specs=pl.BlockSpec((None, td, n_out), lambda l, i: (l, i, 0)),
        out_shape=jax.ShapeDtypeStruct((L, D, n_out), MXU_DT),
        compiler_params=_cparams(("parallel", "parallel")),
        name="relayout_w_in",
    )(w_in)


def _prepare(norm1_g, w_in, fox_fb, fox_qg, fox_kg, conv_dw, conv_db, conv_ln_g, conv_ln_b,
             gla_wa, gla_ba, gla_og, pool_w, pool_scale, gate_b, w_branch, w_out,
             norm2_g, ffn_up, ffn_dw, ffn_db, ffn_down):
    w_cat = _relayout_w_in(w_in)
    row = lambda a: a[:, None, :]
    pad_lanes = lambda a, off: jnp.pad(a, ((0, 0), (off, LANES - off - a.shape[1])))
    wa_pad = jnp.pad(gla_wa, ((0, 0), (SM_GA, LANES - SM_GA - GLA_RANK), (0, 0)))
    return {
        "norm1_g": row(norm1_g),
        "w_cat": w_cat,
        "fox_fb": row(pad_lanes(fox_fb, SM_FF)),
        "fox_qg": row(jnp.tile(fox_qg, (1, FOX_HEADS))),
        "fox_kg": row(jnp.tile(fox_kg, (1, FOX_HEADS))),
        "conv_dw": jnp.pad(conv_dw, ((0, 0), (0, 1), (0, 0))),
        "conv_db": row(conv_db), "conv_ln_g": row(conv_ln_g), "conv_ln_b": row(conv_ln_b),
        "gla_wa": wa_pad.astype(MXU_DT), "gla_ba": row(gla_ba), "gla_og": row(gla_og),
        "pool_w": pool_w.astype(MXU_DT), "pool_scale": row(pool_scale),
        "gate_b": row(gate_b),
        "w_branch": w_branch.astype(MXU_DT),
        "w_out": w_out.astype(MXU_DT),
        "norm2_g": row(norm2_g),
        "ffn_up": ffn_up.astype(MXU_DT), "ffn_dw": ffn_dw, "ffn_db": row(ffn_db),
        "ffn_down": ffn_down.astype(MXU_DT),
    }


def kernel(x, norm1_g, w_in, fox_fb, fox_qg, fox_kg, conv_dw, conv_db, conv_ln_g, conv_ln_b, gla_wa, gla_ba, gla_og, pool_w, pool_scale, gate_b, w_branch, w_out, norm2_g, ffn_up, ffn_dw, ffn_db, ffn_down):
    params = _prepare(norm1_g, w_in, fox_fb, fox_qg, fox_kg, conv_dw, conv_db, conv_ln_g, conv_ln_b,
                      gla_wa, gla_ba, gla_og, pool_w, pool_scale, gate_b, w_branch, w_out,
                      norm2_g, ffn_up, ffn_dw, ffn_db, ffn_down)
    for l in range(w_in.shape[0]):
        x = _layer(x, params, l)
    return x
```

```python
import functools

import jax
import jax.numpy as jnp
from jax import lax
from jax.experimental import pallas as pl
from jax.experimental.pallas import tpu as pltpu

f32 = jnp.float32
MXU_DT = jnp.bfloat16
HI = lax.Precision.HIGHEST

EPS = 1e-6
BRANCH_W = 512
FOX_HEADS = 8
FOX_HD = 64
CONV_K = 31
GLA_HEADS = 4
GLA_DK = 256
GLA_DV = 512
GLA_RANK = 16
GLA_TEMP = 16.0
GLA_CHUNK = 16
POOL_WINDOWS = (2, 4, 8, 16)
FFN_K = 3
LANES = 128

Z_FQ, Z_FK, Z_FV, Z_CA, Z_CG = 0, 512, 1024, 1536, 2048
Z_GQ, Z_GK, Z_GV, Z_GR, Z_PZ, Z_SM, Z_COLS = 2560, 2816, 3072, 3584, 4096, 4608, 5120
Z_USED = Z_SM + 128
SM_FF, SM_GA = 0, 8

NEG_BIG = -1e30
LOG2E = 1.4426950408889634
VMEM_LIMIT = 56 * 1024 * 1024


def _cparams(sem):
    return pltpu.CompilerParams(dimension_semantics=sem, vmem_limit_bytes=VMEM_LIMIT)


def _group(idx, size):
    assert size & (size - 1) == 0
    return idx >> (size.bit_length() - 1)


def _log_sigmoid(x):
    return jnp.minimum(x, 0.0) - jnp.log1p(jnp.exp(-jnp.abs(x)))


def _inproj_body(x_ref, g_ref, w_ref, h_ref, z_ref):
    x = x_ref[...]
    ms = jnp.mean(x * x, axis=-1, keepdims=True)
    h = (x * lax.rsqrt(ms + EPS) * g_ref[...]).astype(h_ref.dtype)
    h_ref[...] = h
    z_ref[...] = jnp.dot(h, w_ref[...], preferred_element_type=f32)


def _inproj(x2, g, w_cat, l):
    T, D = x2.shape
    tm = min(512, T)
    return pl.pallas_call(
        _inproj_body,
        grid=(T // tm,),
        in_specs=[pl.BlockSpec((tm, D), lambda i: (i, 0)),
                  pl.BlockSpec((1, D), lambda i: (0, 0)),
                  pl.BlockSpec((None, D, Z_USED), lambda i: (l, 0, 0), pipeline_mode=pl.Buffered(1))],
        out_specs=[pl.BlockSpec((tm, D), lambda i: (i, 0)),
                   pl.BlockSpec((tm, Z_USED), lambda i: (i, 0))],
        out_shape=[jax.ShapeDtypeStruct((T, D), MXU_DT),
                   jax.ShapeDtypeStruct((T, Z_USED), f32)],
        compiler_params=_cparams(("parallel",)),
        name="inproj",
    )(x2, g, w_cat)


def _head_rmsnorm(x, g, lo):
    sq = x * x
    s0 = jnp.sum(jnp.where(lo, sq, 0.0), axis=-1, keepdims=True)
    s1 = jnp.sum(jnp.where(lo, 0.0, sq), axis=-1, keepdims=True)
    r = jnp.where(lo, lax.rsqrt(s0 * (1.0 / FOX_HD) + EPS), lax.rsqrt(s1 * (1.0 / FOX_HD) + EPS))
    return x * r * g


N_SPLIT = 3
L_KC = FOX_HD
L_QC = FOX_HD + N_SPLIT
L_ONE = FOX_HD


def _split_pieces(x):
    pieces, rest = [], x
    for _ in range(N_SPLIT):
        piece = rest.astype(jnp.bfloat16)
        pieces.append(piece)
        rest = rest - piece.astype(f32)
    return pieces


def _foxprep_body(q_ref, k_ref, v_ref, sm_ref, fb_ref, qg_ref, kg_ref,
                  qt_ref, kt_ref, vt_ref, carry_ref, *, tp):
    i = pl.program_id(1)

    @pl.when(i == 0)
    def _():
        carry_ref[...] = jnp.zeros_like(carry_ref)

    logf = _log_sigmoid(sm_ref[...] + fb_ref[...])
    row = lax.broadcasted_iota(jnp.int32, (tp, tp), 0)
    col = lax.broadcasted_iota(jnp.int32, (tp, tp), 1)
    tri = (col <= row).astype(f32)
    c = jnp.dot(tri, logf, preferred_element_type=f32, precision=HI) + carry_ref[0:1, :]
    carry_ref[0:1, :] = c[tp - 1:tp, :]

    lane = lax.broadcasted_iota(jnp.int32, (1, LANES), 1)
    lo = lane < FOX_HD
    ones_q = ((lane >= L_KC) & (lane < L_KC + N_SPLIT)).astype(f32)
    ones_k = ((lane >= L_QC) & (lane < L_QC + N_SPLIT)).astype(f32)
    ones_v = (lane == L_ONE).astype(f32)
    in_qc = (lane >= L_QC) & (lane < L_QC + N_SPLIT)
    in_kc = (lane >= L_KC) & (lane < L_KC + N_SPLIT)
    pieces = [p.astype(f32) for p in _split_pieces(c * LOG2E)]
    scale = FOX_HD ** -0.5 * LOG2E
    for pair in range(BRANCH_W // LANES):
        sl = slice(pair * LANES, (pair + 1) * LANES)
        qn = _head_rmsnorm(q_ref[:, sl], qg_ref[:, sl], lo) * scale
        kn = _head_rmsnorm(k_ref[:, sl], kg_ref[:, sl], lo)
        vv = v_ref[:, sl]
        for e in (0, 1):
            h = 2 * pair + e
            if e == 1:
                qn, kn, vv = (pltpu.roll(t, FOX_HD, axis=1) for t in (qn, kn, vv))
            cp = jnp.zeros((tp, LANES), f32)
            for j, piece in enumerate(pieces):
                pj = jnp.broadcast_to(piece[:, h:h + 1], (tp, LANES))
                cp = jnp.where((lane == L_KC + j) | (lane == L_QC + j), pj, cp)
            qt_ref[h] = jnp.where(lo, qn, jnp.where(in_qc, cp, ones_q)).astype(qt_ref.dtype)
            kt_ref[h] = jnp.where(lo, kn, jnp.where(in_kc, -cp, ones_k)).astype(kt_ref.dtype)
            vt_ref[h] = jnp.where(lo, vv, ones_v).T.astype(vt_ref.dtype)


def _foxprep(z3, fb_pad, qg, kg):
    B, S, _ = z3.shape
    tp = min(512, S)
    blk = lambda w, c: pl.BlockSpec((None, tp, w), lambda b, i, c=c: (b, i, c))
    vec = lambda w: pl.BlockSpec((1, w), lambda b, i: (0, 0))
    head_blk = pl.BlockSpec((None, FOX_HEADS, tp, LANES), lambda b, i: (b, 0, i, 0))
    return pl.pallas_call(
        functools.partial(_foxprep_body, tp=tp),
        grid=(B, S // tp),
        in_specs=[blk(BRANCH_W, Z_FQ // BRANCH_W), blk(BRANCH_W, Z_FK // BRANCH_W),
                  blk(BRANCH_W, Z_FV // BRANCH_W), blk(LANES, Z_SM // LANES),
                  vec(LANES), vec(BRANCH_W), vec(BRANCH_W)],
        out_specs=[head_blk, head_blk,
                   pl.BlockSpec((None, FOX_HEADS, LANES, tp), lambda b, i: (b, 0, 0, i))],
        out_shape=[jax.ShapeDtypeStruct((B, FOX_HEADS, S, LANES), MXU_DT)] * 2
        + [jax.ShapeDtypeStruct((B, FOX_HEADS, LANES, S), MXU_DT)],
        scratch_shapes=[pltpu.VMEM((8, LANES), f32)],
        compiler_params=_cparams(("parallel", "arbitrary")),
        name="foxprep",
    )(z3, z3, z3, z3, fb_pad, qg, kg)


def _fox_body(q_ref, k_ref, v_ref, o_ref, m_ref, acc_ref, *, tq):
    i = pl.program_id(2)
    m_ref[...] = jnp.full_like(m_ref, NEG_BIG)
    acc_ref[...] = jnp.zeros_like(acc_ref)

    heads = range(FOX_GROUP)

    def blocks(j, masked):
        start = pl.multiple_of(j * tq, tq)
        s = [lax.dot_general(k_ref[g, pl.ds(start, tq), :], q_ref[g], (((1,), (1,)), ((), ())),
                             preferred_element_type=f32) for g in heads]
        if masked:
            key = lax.broadcasted_iota(jnp.int32, (tq, tq), 0)
            qry = lax.broadcasted_iota(jnp.int32, (tq, tq), 1)
            s = [jnp.where(key <= qry, sg, NEG_BIG) for sg in s]
        m_prev = [m_ref[g, 0:1, :] for g in heads]
        m_new = [jnp.maximum(m_prev[g], jnp.max(s[g], axis=0, keepdims=True)) for g in heads]
        p = [jnp.exp2(s[g] - m_new[g]).astype(q_ref.dtype) for g in heads]
        pv = [jnp.dot(v_ref[g, :, pl.ds(start, tq)], p[g], preferred_element_type=f32) for g in heads]
        for g in heads:
            m_ref[g, 0:1, :] = m_new[g]
            acc_ref[g] = jnp.exp2(m_prev[g] - m_new[g]) * acc_ref[g] + pv[g]

    def two_full_blocks(jj, carry):
        blocks(2 * jj, False)
        blocks(2 * jj + 1, False)
        return carry

    lax.fori_loop(0, i // 2, two_full_blocks, 0)

    @pl.when(i % 2 == 1)
    def _():
        blocks(i - 1, False)

    blocks(i, True)
    for pair in range(FOX_GROUP // 2):
        o_t = jnp.concatenate(
            [acc_ref[g, 0:FOX_HD, :] / acc_ref[g, L_ONE:L_ONE + 1, :] for g in (2 * pair, 2 * pair + 1)], axis=0)
        o_ref[:, pair * LANES:(pair + 1) * LANES] = o_t.T.astype(o_ref.dtype)


FOX_GROUP = 8


def _fox_attention(qt, kt, vtt):
    B, H, S, _ = qt.shape
    tq = min(256, S)
    G = FOX_GROUP
    assert G == H
    resident = dict(pipeline_mode=pl.Buffered(1))
    return pl.pallas_call(
        functools.partial(_fox_body, tq=tq),
        grid=(B, H // G, S // tq),
        in_specs=[pl.BlockSpec((None, G, tq, LANES), lambda b, h, i: (b, h, i, 0)),
                  pl.BlockSpec((None, G, S, LANES), lambda b, h, i: (b, h, 0, 0), **resident),
                  pl.BlockSpec((None, G, LANES, S), lambda b, h, i: (b, h, 0, 0), **resident)],
        out_specs=pl.BlockSpec((None, tq, H * FOX_HD), lambda b, h, i: (b, i, 0)),
        out_shape=jax.ShapeDtypeStruct((B, S, H * FOX_HD), MXU_DT),
        scratch_shapes=[pltpu.VMEM((G, 8, tq), f32), pltpu.VMEM((G, LANES, tq), f32)],
        compiler_params=_cparams(("parallel", "parallel", "arbitrary")),
        name="fox_attn",
    )(qt, kt, vtt)


CONV_HALO = 32
CONV_ROWS = 64


def _convmix_body(a_ref, g_ref, dw_ref, db_ref, lng_ref, lnb_ref, o_ref, u_ref, us_ref, *, tp):
    @pl.when(pl.program_id(1) == 0)
    def _():
        u_ref[0:CONV_HALO, :] = jnp.zeros((CONV_HALO, BRANCH_W), f32)

    u_ref[CONV_HALO:CONV_HALO + tp, :] = a_ref[...] * jax.nn.sigmoid(g_ref[...])
    n_shift = CONV_HALO + tp - 8
    for rho in range(1, 8):
        us_ref[rho - 1, 0:n_shift, :] = u_ref[rho:rho + n_shift, :]
    for r in range(0, tp, CONV_ROWS):
        acc = jnp.broadcast_to(db_ref[...], (CONV_ROWS, BRANCH_W))
        for k in range(CONV_K):
            q8, rho = divmod(CONV_HALO - (CONV_K - 1) + k, 8)
            off = r + 8 * q8
            tap = u_ref[off:off + CONV_ROWS, :] if rho == 0 else us_ref[rho - 1, off:off + CONV_ROWS, :]
            acc = acc + dw_ref[k:k + 1, :] * tap
        mu = jnp.mean(acc, axis=-1, keepdims=True)
        d = acc - mu
        var = jnp.mean(d * d, axis=-1, keepdims=True)
        y = d * lax.rsqrt(var + EPS) * lng_ref[...] + lnb_ref[...]
        o_ref[r:r + CONV_ROWS, :] = (y * jax.nn.sigmoid(y)).astype(o_ref.dtype)
    u_ref[0:CONV_HALO, :] = u_ref[tp:tp + CONV_HALO, :]


def _convmix(z3, dw_pad, db, lng, lnb):
    B, S, _ = z3.shape
    tp = min(256, S)
    blk = lambda c: pl.BlockSpec((None, tp, BRANCH_W), lambda b, i, c=c: (b, i, c))
    vec = pl.BlockSpec((1, BRANCH_W), lambda b, i: (0, 0))
    return pl.pallas_call(
        functools.partial(_convmix_body, tp=tp),
        grid=(B, S // tp),
        in_specs=[blk(Z_CA // BRANCH_W), blk(Z_CG // BRANCH_W),
                  pl.BlockSpec(dw_pad.shape, lambda b, i: (0, 0)), vec, vec, vec],
        out_specs=blk(0),
        out_shape=jax.ShapeDtypeStruct((B, S, BRANCH_W), MXU_DT),
        scratch_shapes=[pltpu.VMEM((CONV_HALO + tp, BRANCH_W), f32),
                        pltpu.VMEM((7, CONV_HALO + tp, BRANCH_W), f32)],
        compiler_params=_cparams(("parallel", "arbitrary")),
        name="convmix",
    )(z3, z3, dw_pad, db, lng, lnb)


POOL_HALO = 16


def _poolmix_body(u_in_ref, pw_ref, sc_ref, o_ref, u_ref, *, tp):
    i = pl.program_id(1)

    @pl.when(i == 0)
    def _():
        u_ref[0:POOL_HALO, :] = jnp.zeros((POOL_HALO, BRANCH_W), f32)

    u_ref[POOL_HALO:POOL_HALO + tp, :] = u_in_ref[...]
    pos = i * tp + lax.broadcasted_iota(jnp.int32, (tp, 1), 0)
    for gi, w in enumerate(POOL_WINDOWS):
        sl = slice(gi * LANES, (gi + 1) * LANES)
        acc = u_ref[POOL_HALO:POOL_HALO + tp, sl]
        for j in range(1, w):
            acc = acc + u_ref[POOL_HALO - j:POOL_HALO - j + tp, sl]
        cnt = jnp.minimum(pos + 1, w).astype(f32)
        mixed = acc / cnt - u_ref[POOL_HALO:POOL_HALO + tp, sl]
        out = jnp.dot(mixed.astype(pw_ref.dtype), pw_ref[gi], preferred_element_type=f32)
        o_ref[:, sl] = (out * sc_ref[:, sl]).astype(o_ref.dtype)
    u_ref[0:POOL_HALO, :] = u_ref[tp:tp + POOL_HALO, :]


def _poolmix(z3, pw, scale):
    B, S, _ = z3.shape
    tp = min(512, S)
    return pl.pallas_call(
        functools.partial(_poolmix_body, tp=tp),
        grid=(B, S // tp),
        in_specs=[pl.BlockSpec((None, tp, BRANCH_W), lambda b, i: (b, i, Z_PZ // BRANCH_W)),
                  pl.BlockSpec(pw.shape, lambda b, i: (0, 0, 0)),
                  pl.BlockSpec((1, BRANCH_W), lambda b, i: (0, 0))],
        out_specs=pl.BlockSpec((None, tp, BRANCH_W), lambda b, i: (b, i, 0)),
        out_shape=jax.ShapeDtypeStruct((B, S, BRANCH_W), MXU_DT),
        scratch_shapes=[pltpu.VMEM((POOL_HALO + tp, BRANCH_W), f32)],
        compiler_params=_cparams(("parallel", "arbitrary")),
        name="poolmix",
    )(z3, pw, scale)


def _gla_body(q_ref, k_ref, v_ref, r_ref, sm_ref, wa_ref, ba_ref, og_ref, o_ref,
              st_ref, kb_ref, bb_ref, vb_ref, qlo_ref, qhi_ref, kd_ref, vh_ref, dec_ref, oi_ref, *, tg):
    C = GLA_CHUNK
    hk = GLA_DK // GLA_HEADS
    hv = GLA_DV // GLA_HEADS

    @pl.when(pl.program_id(1) == 0)
    def _():
        st_ref[...] = jnp.zeros_like(st_ref)

    pre = jnp.dot(sm_ref[...].astype(wa_ref.dtype), wa_ref[...], preferred_element_type=f32) + ba_ref[...]
    loga = _log_sigmoid(pre) * (LOG2E / GLA_TEMP)
    row = lax.broadcasted_iota(jnp.int32, (tg, tg), 0)
    col = lax.broadcasted_iota(jnp.int32, (tg, tg), 1)
    same = _group(row, C) == _group(col, C)
    bc = jnp.dot((same & (col <= row)).astype(f32), loga, preferred_element_type=f32, precision=HI)
    blast = jnp.dot(same.astype(f32), loga, preferred_element_type=f32, precision=HI)

    q = q_ref[...] * (hk ** -0.5)
    k = k_ref[...]
    qd = q * jnp.exp2(bc)
    first = (lax.broadcasted_iota(jnp.int32, (1, GLA_DK), 1) & (LANES - 1)) < hk
    qlo_ref[...] = jnp.where(first, qd, 0.0).astype(qlo_ref.dtype)
    qhi_ref[...] = jnp.where(first, 0.0, qd).astype(qhi_ref.dtype)
    kd_ref[...] = (k * jnp.exp2(blast - bc)).astype(kd_ref.dtype)
    dec_ref[...] = jnp.exp2(blast)
    vh_ref[...] = v_ref[...].astype(vh_ref.dtype)

    kb_ref[0:C, :] = jnp.zeros((C, GLA_DK), f32)
    bb_ref[0:C, :] = jnp.zeros((C, GLA_DK), f32)
    vb_ref[0:C, :] = jnp.zeros((C, GLA_DV), f32)
    kb_ref[C:C + tg, :] = k
    bb_ref[C:C + tg, :] = bc
    vb_ref[C:C + tg, :] = v_ref[...]
    rb = (_group(lax.broadcasted_iota(jnp.int32, (GLA_DK, GLA_DV), 0), hk)
          == _group(lax.broadcasted_iota(jnp.int32, (GLA_DK, GLA_DV), 1), hv)).astype(MXU_DT)
    rpos = lax.broadcasted_iota(jnp.int32, (tg, 1), 0) & (C - 1)
    lo = lax.broadcasted_iota(jnp.int32, (1, LANES), 1) < hk
    st = [st_ref[p] for p in range(GLA_HEADS // 2)]

    def intra(delta, acc):
        valid = rpos >= delta
        ks = kb_ref[C - delta:C - delta + tg, :]
        bs = bb_ref[C - delta:C - delta + tg, :]
        vs = vb_ref[C - delta:C - delta + tg, :]
        w = jnp.where(valid, q * ks * jnp.exp2(jnp.where(valid, bc - bs, 0.0)), 0.0)
        return acc + jnp.dot(w.astype(MXU_DT), rb, preferred_element_type=f32) * vs

    def recurrence(n):
        rows = slice(n * C, (n + 1) * C)
        for p in range(GLA_HEADS // 2):
            ps = slice(p * LANES, (p + 1) * LANES)
            lhs = jnp.concatenate([qlo_ref[rows, ps], qhi_ref[rows, ps]], axis=0)
            o_p = lax.dot_general(lhs, st[p].astype(MXU_DT), (((1,), (1,)), ((), ())),
                                  preferred_element_type=f32)
            upd = []
            for e in (0, 1):
                vs = slice((2 * p + e) * hv, (2 * p + e + 1) * hv)
                oi_ref[rows, vs] = o_p[e * C:(e + 1) * C]
                upd.append(lax.dot_general(vh_ref[rows, vs], kd_ref[rows, ps], (((0,), (0,)), ((), ())),
                                           preferred_element_type=f32))
            st[p] = st[p] * dec_ref[n * C:n * C + 1, ps] + jnp.where(lo, upd[0], upd[1])

    n_steps = tg // C
    o_intra = jnp.zeros((tg, GLA_DV), f32)
    for it in range(max(C, n_steps)):
        if it < C:
            o_intra = intra(it, o_intra)
        if it < n_steps:
            recurrence(it)
    for p in range(GLA_HEADS // 2):
        st_ref[p] = st[p]
    oi_ref[...] += o_intra

    for h in range(GLA_HEADS):
        sl = slice(h * hv, (h + 1) * hv)
        oh = oi_ref[:, sl]
        oh = oh * lax.rsqrt(jnp.mean(oh * oh, axis=-1, keepdims=True) + EPS) * og_ref[:, sl]
        rr = r_ref[:, sl]
        o_ref[:, sl] = (oh * (rr * jax.nn.sigmoid(rr))).astype(o_ref.dtype)


def _glamix(z3, wa_pad, ba, og):
    B, S, _ = z3.shape
    tg = min(256, S)
    blk = lambda w, c: pl.BlockSpec((None, tg, w), lambda b, i, c=c: (b, i, c))
    vec = lambda w: pl.BlockSpec((1, w), lambda b, i: (0, 0))
    return pl.pallas_call(
        functools.partial(_gla_body, tg=tg),
        grid=(B, S // tg),
        in_specs=[blk(GLA_DK, Z_GQ // GLA_DK), blk(GLA_DK, Z_GK // GLA_DK),
                  blk(GLA_DV, Z_GV // GLA_DV), blk(GLA_DV, Z_GR // GLA_DV),
                  blk(LANES, Z_SM // LANES),
                  pl.BlockSpec((LANES, GLA_DK), lambda b, i: (0, 0)), vec(GLA_DK), vec(GLA_DV)],
        out_specs=blk(GLA_DV, 0),
        out_shape=jax.ShapeDtypeStruct((B, S, GLA_DV), MXU_DT),
        scratch_shapes=[pltpu.VMEM((GLA_HEADS // 2, GLA_DV // GLA_HEADS, LANES), f32),
                        pltpu.VMEM((GLA_CHUNK + tg, GLA_DK), f32),
                        pltpu.VMEM((GLA_CHUNK + tg, GLA_DK), f32),
                        pltpu.VMEM((GLA_CHUNK + tg, GLA_DV), f32),
                        pltpu.VMEM((tg, GLA_DK), MXU_DT),
                        pltpu.VMEM((tg, GLA_DK), MXU_DT),
                        pltpu.VMEM((tg, GLA_DK), MXU_DT),
                        pltpu.VMEM((tg, GLA_DV), MXU_DT),
                        pltpu.VMEM((tg, GLA_DK), f32),
                        pltpu.VMEM((tg, GLA_DV), f32)],
        compiler_params=_cparams(("parallel", "arbitrary")),
        name="glamix",
    )(z3, z3, z3, z3, z3, wa_pad, ba, og)


def _merge_body(h_ref, oa_ref, ob_ref, oc_ref, od_ref, wg0, wg1, wg2, wg3, gb0, gb1, gb2, gb3,
                p0, p1, p2, p3, y_ref):
    h = h_ref[...]
    acc = None
    for o_ref, wg, gb, p in ((oa_ref, wg0, gb0, p0), (ob_ref, wg1, gb1, p1),
                             (oc_ref, wg2, gb2, p2), (od_ref, wg3, gb3, p3)):
        gate = jax.nn.sigmoid(jnp.dot(h, wg[...], preferred_element_type=f32) + gb[...])
        term = gate * jnp.dot(o_ref[...], p[...], preferred_element_type=f32)
        acc = term if acc is None else acc + term
    y_ref[...] = acc.astype(y_ref.dtype)


def _merge(h, outs, w_gate, gate_b, w_branch, l):
    T, D = h.shape
    tm, tn = min(1024, T), 512
    nj = D // tn
    g0 = Z_COLS // tn
    tok = lambda w: pl.BlockSpec((tm, w), lambda i, j: (i, 0))
    wg = [pl.BlockSpec((None, D, tn), lambda i, j, b=b: (l, 0, g0 + b * nj + j)) for b in range(4)]
    gb = [pl.BlockSpec((1, tn), lambda i, j, b=b: (0, b * nj + j)) for b in range(4)]
    pb = [pl.BlockSpec((None, None, BRANCH_W, tn), lambda i, j, b=b: (l, b, 0, j)) for b in range(4)]
    return pl.pallas_call(
        _merge_body,
        grid=(T // tm, nj),
        in_specs=[tok(D)] + [tok(BRANCH_W)] * 4 + wg + gb + pb,
        out_specs=pl.BlockSpec((tm, tn), lambda i, j: (i, j)),
        out_shape=jax.ShapeDtypeStruct((T, D), MXU_DT),
        compiler_params=_cparams(("parallel", "arbitrary")),
        name="merge",
    )(h, *outs, *([w_gate] * 4), *([gate_b] * 4), *([w_branch] * 4))


def _outproj_body(y_ref, x_ref, w_ref, g_ref, x1_ref, h2_ref):
    x1 = x_ref[...] + jnp.dot(y_ref[...], w_ref[...], preferred_element_type=f32)
    x1_ref[...] = x1
    ms = jnp.mean(x1 * x1, axis=-1, keepdims=True)
    h2_ref[...] = (x1 * lax.rsqrt(ms + EPS) * g_ref[...]).astype(h2_ref.dtype)


def _outproj(y, x2, w_out, g2, l):
    T, D = x2.shape
    tm = min(512, T)
    tok = pl.BlockSpec((tm, D), lambda i: (i, 0))
    return pl.pallas_call(
        _outproj_body,
        grid=(T // tm,),
        in_specs=[tok, tok, pl.BlockSpec((None, D, D), lambda i: (l, 0, 0)),
                  pl.BlockSpec((1, D), lambda i: (0, 0))],
        out_specs=[tok, tok],
        out_shape=[jax.ShapeDtypeStruct((T, D), f32), jax.ShapeDtypeStruct((T, D), MXU_DT)],
        compiler_params=_cparams(("parallel",)),
        name="outproj",
    )(y, x2, w_out, g2)


FFN_HALO = 8


def _ffnup_body(h_ref, wa_ref, wv_ref, dwa_ref, dwv_ref, dba_ref, dbv_ref, g_ref, ua_ref, uv_ref,
                *, tm, tiles_per_seq):
    @pl.when(pl.program_id(1) % tiles_per_seq == 0)
    def _():
        ua_ref[0:FFN_HALO, :] = jnp.zeros((FFN_HALO, ua_ref.shape[1]), f32)
        uv_ref[0:FFN_HALO, :] = jnp.zeros((FFN_HALO, uv_ref.shape[1]), f32)

    h = h_ref[...]

    def conv(w_ref, dw_ref, db_ref, u_ref):
        u_ref[FFN_HALO:FFN_HALO + tm, :] = jnp.dot(h, w_ref[...], preferred_element_type=f32)
        y = db_ref[...] + dw_ref[FFN_K - 1:FFN_K, :] * u_ref[FFN_HALO:FFN_HALO + tm, :]
        for k in range(FFN_K - 1):
            off = FFN_HALO - (FFN_K - 1) + k
            y = y + dw_ref[k:k + 1, :] * u_ref[off:off + tm, :]
        u_ref[0:FFN_HALO, :] = u_ref[tm:tm + FFN_HALO, :]
        return y

    a = conv(wa_ref, dwa_ref, dba_ref, ua_ref)
    v = conv(wv_ref, dwv_ref, dbv_ref, uv_ref)
    g_ref[...] = (a * jax.nn.sigmoid(a) * v).astype(g_ref.dtype)


def _ffnup(h2, ffn_up, ffn_dw, ffn_db, seq_len, l):
    T, D = h2.shape
    dff = ffn_up.shape[2] // 2
    tm, tn = min(1024, seq_len), 512
    nj = dff // tn
    return pl.pallas_call(
        functools.partial(_ffnup_body, tm=tm, tiles_per_seq=seq_len // tm),
        grid=(nj, T // tm),
        in_specs=[pl.BlockSpec((tm, D), lambda j, i: (i, 0)),
                  pl.BlockSpec((None, D, tn), lambda j, i: (l, 0, j)),
                  pl.BlockSpec((None, D, tn), lambda j, i: (l, 0, nj + j)),
                  pl.BlockSpec((FFN_K, tn), lambda j, i: (0, j)),
                  pl.BlockSpec((FFN_K, tn), lambda j, i: (0, nj + j)),
                  pl.BlockSpec((1, tn), lambda j, i: (0, j)),
                  pl.BlockSpec((1, tn), lambda j, i: (0, nj + j))],
        out_specs=pl.BlockSpec((tm, tn), lambda j, i: (i, j)),
        out_shape=jax.ShapeDtypeStruct((T, dff), MXU_DT),
        scratch_shapes=[pltpu.VMEM((FFN_HALO + tm, tn), f32), pltpu.VMEM((FFN_HALO + tm, tn), f32)],
        compiler_params=_cparams(("parallel", "arbitrary")),
        name="ffnup",
    )(h2, ffn_up, ffn_up, ffn_dw, ffn_dw, ffn_db, ffn_db)


def _ffndown_body(g_ref, w_ref, x_ref, o_ref):
    o_ref[...] = x_ref[...] + jnp.dot(g_ref[...], w_ref[...], preferred_element_type=f32)


def _ffndown(g, ffn_down, x1, l):
    T, D = x1.shape
    dff = g.shape[1]
    tm = min(512, T)
    return pl.pallas_call(
        _ffndown_body,
        grid=(T // tm,),
        in_specs=[pl.BlockSpec((tm, dff), lambda i: (i, 0)),
                  pl.BlockSpec((None, dff, D), lambda i: (l, 0, 0), pipeline_mode=pl.Buffered(1)),
                  pl.BlockSpec((tm, D), lambda i: (i, 0))],
        out_specs=pl.BlockSpec((tm, D), lambda i: (i, 0)),
        out_shape=jax.ShapeDtypeStruct((T, D), f32),
        compiler_params=_cparams(("parallel",)),
        name="ffndown",
    )(g, ffn_down, x1)


def _mixers(z3, p):
    B, S, _ = z3.shape
    qt, kt, vt = _foxprep(z3, p["fox_fb"], p["fox_qg"], p["fox_kg"])
    o_a = _fox_attention(qt, kt, vt)
    o_b = _convmix(z3, p["conv_dw"], p["conv_db"], p["conv_ln_g"], p["conv_ln_b"])
    o_c = _glamix(z3, p["gla_wa"], p["gla_ba"], p["gla_og"])
    o_d = _poolmix(z3, p["pool_w"], p["pool_scale"])
    return o_a, o_b, o_c, o_d


STACKED = ("w_cat", "w_branch", "w_out", "ffn_up", "ffn_down")


def _layer(x, params, l):
    B, S, D = x.shape
    T = B * S
    x2 = x.reshape(T, D)
    p = {k: (v if k in STACKED else v[l]) for k, v in params.items()}
    h, z = _inproj(x2, p["norm1_g"], p["w_cat"], l)
    outs = _mixers(z.reshape(B, S, Z_USED), p)
    y = _merge(h, [o.reshape(T, BRANCH_W) for o in outs], p["w_cat"], p["gate_b"], p["w_branch"], l)
    x1, h2 = _outproj(y, x2, p["w_out"], p["norm2_g"], l)
    g = _ffnup(h2, p["ffn_up"], p["ffn_dw"], p["ffn_db"], S, l)
    return _ffndown(g, p["ffn_down"], x1, l).reshape(B, S, D)


def _w_in_pieces(d_model):
    n_small = 3 * BRANCH_W + FOX_HEADS + 2 * BRANCH_W + 2 * GLA_DK + GLA_DV + GLA_RANK + GLA_DV + BRANCH_W
    o_ff = 3 * BRANCH_W
    o_cz = o_ff + FOX_HEADS
    o_ga = o_cz + 2 * BRANCH_W + 2 * GLA_DK + GLA_DV
    o_gr = o_ga + GLA_RANK
    return n_small, [(0, o_ff, Z_FQ), (o_cz, o_ga, Z_CA), (o_gr, n_small, Z_GR),
                     (o_ff, o_cz, Z_SM + SM_FF), (o_ga, o_gr, Z_SM + SM_GA),
                     (n_small, n_small + 4 * d_model, Z_COLS)]


RELAYOUT_CHUNK = 512


def _relayout_body(w_ref, o_ref, *, pieces):
    td = o_ref.shape[0]
    narrow = []
    for src0, src1, dst in pieces:
        if (src1 - src0) % RELAYOUT_CHUNK:
            narrow.append((src0, src1, dst))
            continue
        for c in range(0, src1 - src0, RELAYOUT_CHUNK):
            o_ref[:, dst + c:dst + c + RELAYOUT_CHUNK] = (
                w_ref[src0 + c:src0 + c + RELAYOUT_CHUNK, :].T.astype(o_ref.dtype))
    assert narrow[0][2] == Z_SM and all(a[2] + a[1] - a[0] == b[2] for a, b in zip(narrow, narrow[1:]))
    used = sum(s1 - s0 for s0, s1, _ in narrow)
    rows = [w_ref[s0:s1, :] for s0, s1, _ in narrow] + [jnp.zeros((LANES - used, td), f32)]
    o_ref[:, Z_SM:Z_SM + LANES] = jnp.concatenate(rows, axis=0).T.astype(o_ref.dtype)
    o_ref[:, Z_SM + LANES:Z_COLS] = jnp.zeros((td, Z_COLS - Z_SM - LANES), o_ref.dtype)


def _relayout_w_in(w_in):
    L, D, n_in = w_in.shape
    _, pieces = _w_in_pieces(D)
    n_out = Z_COLS + 4 * D
    td = 256
    return pl.pallas_call(
        functools.partial(_relayout_body, pieces=pieces),
        grid=(L, D // td),
        in_specs=[pl.BlockSpec((None, n_in, td), lambda l, i: (l, 0, i))],
        out_specs=pl.BlockSpec((None, td, n_out), lambda l, i: (l, i, 0)),
        out_shape=jax.ShapeDtypeStruct((L, D, n_out), MXU_DT),
        compiler_params=_cparams(("parallel", "parallel")),
        name="relayout_w_in",
    )(jnp.swapaxes(w_in, 1, 2))


def _prepare(norm1_g, w_in, fox_fb, fox_qg, fox_kg, conv_dw, conv_db, conv_ln_g, conv_ln_b,
             gla_wa, gla_ba, gla_og, pool_w, pool_scale, gate_b, w_branch, w_out,
             norm2_g, ffn_up, ffn_dw, ffn_db, ffn_down):
    w_cat = _relayout_w_in(w_in)
    row = lambda a: a[:, None, :]
    pad_lanes = lambda a, off: jnp.pad(a, ((0, 0), (off, LANES - off - a.shape[1])))
    wa_pad = jnp.pad(gla_wa, ((0, 0), (SM_GA, LANES - SM_GA - GLA_RANK), (0, 0)))
    return {
        "norm1_g": row(norm1_g),
        "w_cat": w_cat,
        "fox_fb": row(pad_lanes(fox_fb, SM_FF)),
        "fox_qg": row(jnp.tile(fox_qg, (1, FOX_HEADS))),
        "fox_kg": row(jnp.tile(fox_kg, (1, FOX_HEADS))),
        "conv_dw": jnp.pad(conv_dw, ((0, 0), (0, 1), (0, 0))),
        "conv_db": row(conv_db), "conv_ln_g": row(conv_ln_g), "conv_ln_b": row(conv_ln_b),
        "gla_wa": wa_pad.astype(MXU_DT), "gla_ba": row(gla_ba), "gla_og": row(gla_og),
        "pool_w": pool_w.astype(MXU_DT), "pool_scale": row(pool_scale),
        "gate_b": row(gate_b),
        "w_branch": w_branch.astype(MXU_DT),
        "w_out": w_out.astype(MXU_DT),
        "norm2_g": row(norm2_g),
        "ffn_up": ffn_up.astype(MXU_DT), "ffn_dw": ffn_dw, "ffn_db": row(ffn_db),
        "ffn_down": ffn_down.astype(MXU_DT),
    }


def kernel(x, norm1_g, w_in, fox_fb, fox_qg, fox_kg, conv_dw, conv_db, conv_ln_g, conv_ln_b, gla_wa, gla_ba, gla_og, pool_w, pool_scale, gate_b, w_branch, w_out, norm2_g, ffn_up, ffn_dw, ffn_db, ffn_down):
    params = _prepare(norm1_g, w_in, fox_fb, fox_qg, fox_kg, conv_dw, conv_db, conv_ln_g, conv_ln_b,
                      gla_wa, gla_ba, gla_og, pool_w, pool_scale, gate_b, w_branch, w_out,
                      norm2_g, ffn_up, ffn_dw, ffn_db, ffn_down)
    for l in range(w_in.shape[0]):
        x = _layer(x, params, l)
    return x
```

```python
import functools

import jax
import jax.numpy as jnp
from jax import lax
from jax.experimental import pallas as pl
from jax.experimental.pallas import tpu as pltpu

f32 = jnp.float32
MXU_DT = jnp.bfloat16
HI = lax.Precision.HIGHEST

EPS = 1e-6
BRANCH_W = 512
FOX_HEADS = 8
FOX_HD = 64
CONV_K = 31
GLA_HEADS = 4
GLA_DK = 256
GLA_DV = 512
GLA_RANK = 16
GLA_TEMP = 16.0
GLA_CHUNK = 16
POOL_WINDOWS = (2, 4, 8, 16)
FFN_K = 3
LANES = 128

Z_FQ, Z_FK, Z_FV, Z_CA, Z_CG = 0, 512, 1024, 1536, 2048
Z_GQ, Z_GK, Z_GV, Z_GR, Z_PZ, Z_SM, Z_COLS = 2560, 2816, 3072, 3584, 4096, 4608, 5120
Z_USED = Z_SM + 128
SM_FF, SM_GA = 0, 8

NEG_BIG = -1e30
LOG2E = 1.4426950408889634
VMEM_LIMIT = 56 * 1024 * 1024


def _cparams(sem):
    return pltpu.CompilerParams(dimension_semantics=sem, vmem_limit_bytes=VMEM_LIMIT)


def _group(idx, size):
    assert size & (size - 1) == 0
    return idx >> (size.bit_length() - 1)


def _log_sigmoid(x):
    return jnp.minimum(x, 0.0) - jnp.log1p(jnp.exp(-jnp.abs(x)))


def _inproj_body(x_ref, g_ref, w_ref, h_ref, z_ref):
    x = x_ref[...]
    ms = jnp.mean(x * x, axis=-1, keepdims=True)
    h = (x * lax.rsqrt(ms + EPS) * g_ref[...]).astype(h_ref.dtype)
    h_ref[...] = h
    z_ref[...] = jnp.dot(h, w_ref[...], preferred_element_type=f32)


def _inproj(x2, g, w_cat, l):
    T, D = x2.shape
    tm = min(512, T)
    return pl.pallas_call(
        _inproj_body,
        grid=(T // tm,),
        in_specs=[pl.BlockSpec((tm, D), lambda i: (i, 0)),
                  pl.BlockSpec((1, D), lambda i: (0, 0)),
                  pl.BlockSpec((None, D, Z_USED), lambda i: (l, 0, 0), pipeline_mode=pl.Buffered(1))],
        out_specs=[pl.BlockSpec((tm, D), lambda i: (i, 0)),
                   pl.BlockSpec((tm, Z_USED), lambda i: (i, 0))],
        out_shape=[jax.ShapeDtypeStruct((T, D), MXU_DT),
                   jax.ShapeDtypeStruct((T, Z_USED), f32)],
        compiler_params=_cparams(("parallel",)),
        name="inproj",
    )(x2, g, w_cat)


def _head_rmsnorm(x, g, lo):
    sq = x * x
    s0 = jnp.sum(jnp.where(lo, sq, 0.0), axis=-1, keepdims=True)
    s1 = jnp.sum(jnp.where(lo, 0.0, sq), axis=-1, keepdims=True)
    r = jnp.where(lo, lax.rsqrt(s0 * (1.0 / FOX_HD) + EPS), lax.rsqrt(s1 * (1.0 / FOX_HD) + EPS))
    return x * r * g


N_SPLIT = 3
L_KC = FOX_HD
L_QC = FOX_HD + N_SPLIT
L_ONE = FOX_HD
V_ROWS = FOX_HD + 16


def _split_pieces(x):
    pieces, rest = [], x
    for _ in range(N_SPLIT):
        piece = rest.astype(jnp.bfloat16)
        pieces.append(piece)
        rest = rest - piece.astype(f32)
    return pieces


def _foxprep_body(q_ref, k_ref, v_ref, sm_ref, fb_ref, qg_ref, kg_ref,
                  qt_ref, kt_ref, vt_ref, carry_ref, *, tp):
    i = pl.program_id(1)

    @pl.when(i == 0)
    def _():
        carry_ref[...] = jnp.zeros_like(carry_ref)

    logf = _log_sigmoid(sm_ref[...] + fb_ref[...])
    row = lax.broadcasted_iota(jnp.int32, (tp, tp), 0)
    col = lax.broadcasted_iota(jnp.int32, (tp, tp), 1)
    tri = (col <= row).astype(f32)
    c = jnp.dot(tri, logf, preferred_element_type=f32, precision=HI) + carry_ref[0:1, :]
    carry_ref[0:1, :] = c[tp - 1:tp, :]

    lane = lax.broadcasted_iota(jnp.int32, (1, LANES), 1)
    lo = lane < FOX_HD
    ones_q = ((lane >= L_KC) & (lane < L_KC + N_SPLIT)).astype(f32)
    ones_k = ((lane >= L_QC) & (lane < L_QC + N_SPLIT)).astype(f32)
    ones_v = (lane == L_ONE).astype(f32)
    in_qc = (lane >= L_QC) & (lane < L_QC + N_SPLIT)
    in_kc = (lane >= L_KC) & (lane < L_KC + N_SPLIT)
    pieces = [p.astype(f32) for p in _split_pieces(c * LOG2E)]
    scale = FOX_HD ** -0.5 * LOG2E
    for pair in range(BRANCH_W // LANES):
        sl = slice(pair * LANES, (pair + 1) * LANES)
        qn = _head_rmsnorm(q_ref[:, sl], qg_ref[:, sl], lo) * scale
        kn = _head_rmsnorm(k_ref[:, sl], kg_ref[:, sl], lo)
        vv = v_ref[:, sl]
        for e in (0, 1):
            h = 2 * pair + e
            if e == 1:
                qn, kn, vv = (pltpu.roll(t, FOX_HD, axis=1) for t in (qn, kn, vv))
            cp = jnp.zeros((tp, LANES), f32)
            for j, piece in enumerate(pieces):
                pj = jnp.broadcast_to(piece[:, h:h + 1], (tp, LANES))
                cp = jnp.where((lane == L_KC + j) | (lane == L_QC + j), pj, cp)
            qt_ref[h] = jnp.where(lo, qn, jnp.where(in_qc, cp, ones_q)).astype(qt_ref.dtype)
            kt_ref[h] = jnp.where(lo, kn, jnp.where(in_kc, -cp, ones_k)).astype(kt_ref.dtype)
            vt_ref[h] = jnp.where(lo, vv, ones_v).T[0:V_ROWS, :].astype(vt_ref.dtype)


def _foxprep(z3, fb_pad, qg, kg):
    B, S, _ = z3.shape
    tp = min(512, S)
    blk = lambda w, c: pl.BlockSpec((None, tp, w), lambda b, i, c=c: (b, i, c))
    vec = lambda w: pl.BlockSpec((1, w), lambda b, i: (0, 0))
    head_blk = pl.BlockSpec((None, FOX_HEADS, tp, LANES), lambda b, i: (b, 0, i, 0))
    return pl.pallas_call(
        functools.partial(_foxprep_body, tp=tp),
        grid=(B, S // tp),
        in_specs=[blk(BRANCH_W, Z_FQ // BRANCH_W), blk(BRANCH_W, Z_FK // BRANCH_W),
                  blk(BRANCH_W, Z_FV // BRANCH_W), blk(LANES, Z_SM // LANES),
                  vec(LANES), vec(BRANCH_W), vec(BRANCH_W)],
        out_specs=[head_blk, head_blk,
                   pl.BlockSpec((None, FOX_HEADS, V_ROWS, tp), lambda b, i: (b, 0, 0, i))],
        out_shape=[jax.ShapeDtypeStruct((B, FOX_HEADS, S, LANES), MXU_DT)] * 2
        + [jax.ShapeDtypeStruct((B, FOX_HEADS, V_ROWS, S), MXU_DT)],
        scratch_shapes=[pltpu.VMEM((8, LANES), f32)],
        compiler_params=_cparams(("parallel", "arbitrary")),
        name="foxprep",
    )(z3, z3, z3, z3, fb_pad, qg, kg)


def _fox_body(q_ref, k_ref, v_ref, o_ref, m_ref, acc_ref, *, tq):
    i = pl.program_id(2)
    m_ref[...] = jnp.full_like(m_ref, NEG_BIG)
    acc_ref[...] = jnp.zeros_like(acc_ref)

    heads = range(FOX_GROUP)

    def blocks(j, masked):
        start = pl.multiple_of(j * tq, tq)
        s = [lax.dot_general(k_ref[g, pl.ds(start, tq), :], q_ref[g], (((1,), (1,)), ((), ())),
                             preferred_element_type=f32) for g in heads]
        if masked:
            key = lax.broadcasted_iota(jnp.int32, (tq, tq), 0)
            qry = lax.broadcasted_iota(jnp.int32, (tq, tq), 1)
            s = [jnp.where(key <= qry, sg, NEG_BIG) for sg in s]
        m_prev = [m_ref[g, 0:1, :] for g in heads]
        m_new = [jnp.maximum(m_prev[g], jnp.max(s[g], axis=0, keepdims=True)) for g in heads]
        p = [jnp.exp2(s[g] - m_new[g]).astype(q_ref.dtype) for g in heads]
        pv = [jnp.dot(v_ref[g, :, pl.ds(start, tq)], p[g], preferred_element_type=f32) for g in heads]
        for g in heads:
            m_ref[g, 0:1, :] = m_new[g]
            acc_ref[g] = jnp.exp2(m_prev[g] - m_new[g]) * acc_ref[g] + pv[g]

    def full_blocks(jj, carry):
        for u in range(FOX_UNROLL):
            blocks(FOX_UNROLL * jj + u, False)
        return carry

    lax.fori_loop(0, i // FOX_UNROLL, full_blocks, 0)
    for u in range(FOX_UNROLL - 1):
        @pl.when(i % FOX_UNROLL > u)
        def _():
            blocks(i - i % FOX_UNROLL + u, False)

    blocks(i, True)
    for pair in range(FOX_GROUP // 2):
        o_t = jnp.concatenate(
            [acc_ref[g, 0:FOX_HD, :] / acc_ref[g, L_ONE:L_ONE + 1, :] for g in (2 * pair, 2 * pair + 1)], axis=0)
        o_ref[:, pair * LANES:(pair + 1) * LANES] = o_t.T.astype(o_ref.dtype)


FOX_GROUP = 8
FOX_UNROLL = 4


def _fox_attention(qt, kt, vtt):
    B, H, S, _ = qt.shape
    tq = min(256, S)
    G = FOX_GROUP
    assert G == H
    resident = dict(pipeline_mode=pl.Buffered(1))
    return pl.pallas_call(
        functools.partial(_fox_body, tq=tq),
        grid=(B, H // G, S // tq),
        in_specs=[pl.BlockSpec((None, G, tq, LANES), lambda b, h, i: (b, h, i, 0)),
                  pl.BlockSpec((None, G, S, LANES), lambda b, h, i: (b, h, 0, 0), **resident),
                  pl.BlockSpec((None, G, V_ROWS, S), lambda b, h, i: (b, h, 0, 0), **resident)],
        out_specs=pl.BlockSpec((None, tq, H * FOX_HD), lambda b, h, i: (b, i, 0)),
        out_shape=jax.ShapeDtypeStruct((B, S, H * FOX_HD), MXU_DT),
        scratch_shapes=[pltpu.VMEM((G, 8, tq), f32), pltpu.VMEM((G, V_ROWS, tq), f32)],
        compiler_params=_cparams(("parallel", "parallel", "arbitrary")),
        name="fox_attn",
    )(qt, kt, vtt)


CONV_HALO = 32
CONV_ROWS = 64


def _convmix_body(a_ref, g_ref, dw_ref, db_ref, lng_ref, lnb_ref, o_ref, u_ref, us_ref, *, tp):
    @pl.when(pl.program_id(1) == 0)
    def _():
        u_ref[0:CONV_HALO, :] = jnp.zeros((CONV_HALO, BRANCH_W), f32)

    u_ref[CONV_HALO:CONV_HALO + tp, :] = a_ref[...] * jax.nn.sigmoid(g_ref[...])
    n_shift = CONV_HALO + tp - 8
    for rho in range(1, 8):
        us_ref[rho - 1, 0:n_shift, :] = u_ref[rho:rho + n_shift, :]
    for r in range(0, tp, CONV_ROWS):
        acc = jnp.broadcast_to(db_ref[...], (CONV_ROWS, BRANCH_W))
        for k in range(CONV_K):
            q8, rho = divmod(CONV_HALO - (CONV_K - 1) + k, 8)
            off = r + 8 * q8
            tap = u_ref[off:off + CONV_ROWS, :] if rho == 0 else us_ref[rho - 1, off:off + CONV_ROWS, :]
            acc = acc + dw_ref[k:k + 1, :] * tap
        mu = jnp.mean(acc, axis=-1, keepdims=True)
        d = acc - mu
        var = jnp.mean(d * d, axis=-1, keepdims=True)
        y = d * lax.rsqrt(var + EPS) * lng_ref[...] + lnb_ref[...]
        o_ref[r:r + CONV_ROWS, :] = (y * jax.nn.sigmoid(y)).astype(o_ref.dtype)
    u_ref[0:CONV_HALO, :] = u_ref[tp:tp + CONV_HALO, :]


def _convmix(z3, dw_pad, db, lng, lnb):
    B, S, _ = z3.shape
    tp = min(256, S)
    blk = lambda c: pl.BlockSpec((None, tp, BRANCH_W), lambda b, i, c=c: (b, i, c))
    vec = pl.BlockSpec((1, BRANCH_W), lambda b, i: (0, 0))
    return pl.pallas_call(
        functools.partial(_convmix_body, tp=tp),
        grid=(B, S // tp),
        in_specs=[blk(Z_CA // BRANCH_W), blk(Z_CG // BRANCH_W),
                  pl.BlockSpec(dw_pad.shape, lambda b, i: (0, 0)), vec, vec, vec],
        out_specs=blk(0),
        out_shape=jax.ShapeDtypeStruct((B, S, BRANCH_W), MXU_DT),
        scratch_shapes=[pltpu.VMEM((CONV_HALO + tp, BRANCH_W), f32),
                        pltpu.VMEM((7, CONV_HALO + tp, BRANCH_W), f32)],
        compiler_params=_cparams(("parallel", "arbitrary")),
        name="convmix",
    )(z3, z3, dw_pad, db, lng, lnb)


POOL_HALO = 16


def _poolmix_body(u_in_ref, pw_ref, sc_ref, o_ref, u_ref, *, tp):
    i = pl.program_id(1)

    @pl.when(i == 0)
    def _():
        u_ref[0:POOL_HALO, :] = jnp.zeros((POOL_HALO, BRANCH_W), f32)

    u_ref[POOL_HALO:POOL_HALO + tp, :] = u_in_ref[...]
    pos = i * tp + lax.broadcasted_iota(jnp.int32, (tp, 1), 0)
    for gi, w in enumerate(POOL_WINDOWS):
        sl = slice(gi * LANES, (gi + 1) * LANES)
        acc = u_ref[POOL_HALO:POOL_HALO + tp, sl]
        for j in range(1, w):
            acc = acc + u_ref[POOL_HALO - j:POOL_HALO - j + tp, sl]
        cnt = jnp.minimum(pos + 1, w).astype(f32)
        mixed = acc / cnt - u_ref[POOL_HALO:POOL_HALO + tp, sl]
        out = jnp.dot(mixed.astype(pw_ref.dtype), pw_ref[gi], preferred_element_type=f32)
        o_ref[:, sl] = (out * sc_ref[:, sl]).astype(o_ref.dtype)
    u_ref[0:POOL_HALO, :] = u_ref[tp:tp + POOL_HALO, :]


def _poolmix(z3, pw, scale):
    B, S, _ = z3.shape
    tp = min(512, S)
    return pl.pallas_call(
        functools.partial(_poolmix_body, tp=tp),
        grid=(B, S // tp),
        in_specs=[pl.BlockSpec((None, tp, BRANCH_W), lambda b, i: (b, i, Z_PZ // BRANCH_W)),
                  pl.BlockSpec(pw.shape, lambda b, i: (0, 0, 0)),
                  pl.BlockSpec((1, BRANCH_W), lambda b, i: (0, 0))],
        out_specs=pl.BlockSpec((None, tp, BRANCH_W), lambda b, i: (b, i, 0)),
        out_shape=jax.ShapeDtypeStruct((B, S, BRANCH_W), MXU_DT),
        scratch_shapes=[pltpu.VMEM((POOL_HALO + tp, BRANCH_W), f32)],
        compiler_params=_cparams(("parallel", "arbitrary")),
        name="poolmix",
    )(z3, pw, scale)


def _gla_body(q_ref, k_ref, v_ref, r_ref, sm_ref, wa_ref, ba_ref, og_ref, o_ref,
              st_ref, kb_ref, bb_ref, vb_ref, qlo_ref, qhi_ref, kd_ref, vh_ref, dec_ref, oi_ref, *, tg):
    C = GLA_CHUNK
    hk = GLA_DK // GLA_HEADS
    hv = GLA_DV // GLA_HEADS

    @pl.when(pl.program_id(1) == 0)
    def _():
        st_ref[...] = jnp.zeros_like(st_ref)

    pre = jnp.dot(sm_ref[...].astype(wa_ref.dtype), wa_ref[...], preferred_element_type=f32) + ba_ref[...]
    loga = _log_sigmoid(pre) * (LOG2E / GLA_TEMP)
    row = lax.broadcasted_iota(jnp.int32, (tg, tg), 0)
    col = lax.broadcasted_iota(jnp.int32, (tg, tg), 1)
    same = _group(row, C) == _group(col, C)
    bc = jnp.dot((same & (col <= row)).astype(f32), loga, preferred_element_type=f32, precision=HI)
    blast = jnp.dot(same.astype(f32), loga, preferred_element_type=f32, precision=HI)

    q = q_ref[...] * (hk ** -0.5)
    k = k_ref[...]
    qd = q * jnp.exp2(bc)
    first = (lax.broadcasted_iota(jnp.int32, (1, GLA_DK), 1) & (LANES - 1)) < hk
    qlo_ref[...] = jnp.where(first, qd, 0.0).astype(qlo_ref.dtype)
    qhi_ref[...] = jnp.where(first, 0.0, qd).astype(qhi_ref.dtype)
    kd_ref[...] = (k * jnp.exp2(blast - bc)).astype(kd_ref.dtype)
    dec_ref[...] = jnp.exp2(blast)
    vh_ref[...] = v_ref[...].astype(vh_ref.dtype)

    kb_ref[0:C, :] = jnp.zeros((C, GLA_DK), f32)
    bb_ref[0:C, :] = jnp.zeros((C, GLA_DK), f32)
    vb_ref[0:C, :] = jnp.zeros((C, GLA_DV), f32)
    kb_ref[C:C + tg, :] = k
    bb_ref[C:C + tg, :] = bc
    vb_ref[C:C + tg, :] = v_ref[...]
    rb = (_group(lax.broadcasted_iota(jnp.int32, (GLA_DK, GLA_DV), 0), hk)
          == _group(lax.broadcasted_iota(jnp.int32, (GLA_DK, GLA_DV), 1), hv)).astype(MXU_DT)
    rpos = lax.broadcasted_iota(jnp.int32, (tg, 1), 0) & (C - 1)
    lo = lax.broadcasted_iota(jnp.int32, (1, LANES), 1) < hk
    st = [st_ref[p] for p in range(GLA_HEADS // 2)]

    def intra(delta, acc):
        valid = rpos >= delta
        ks = kb_ref[C - delta:C - delta + tg, :]
        bs = bb_ref[C - delta:C - delta + tg, :]
        vs = vb_ref[C - delta:C - delta + tg, :]
        w = jnp.where(valid, q * ks * jnp.exp2(jnp.where(valid, bc - bs, 0.0)), 0.0)
        return acc + jnp.dot(w.astype(MXU_DT), rb, preferred_element_type=f32) * vs

    def recurrence(n):
        rows = slice(n * C, (n + 1) * C)
        for p in range(GLA_HEADS // 2):
            ps = slice(p * LANES, (p + 1) * LANES)
            lhs = jnp.concatenate([qlo_ref[rows, ps], qhi_ref[rows, ps]], axis=0)
            o_p = lax.dot_general(lhs, st[p].astype(MXU_DT), (((1,), (1,)), ((), ())),
                                  preferred_element_type=f32)
            upd = []
            for e in (0, 1):
                vs = slice((2 * p + e) * hv, (2 * p + e + 1) * hv)
                oi_ref[rows, vs] = o_p[e * C:(e + 1) * C]
                upd.append(lax.dot_general(vh_ref[rows, vs], kd_ref[rows, ps], (((0,), (0,)), ((), ())),
                                           preferred_element_type=f32))
            st[p] = st[p] * dec_ref[n * C:n * C + 1, ps] + jnp.where(lo, upd[0], upd[1])

    n_steps = tg // C
    o_intra = jnp.zeros((tg, GLA_DV), f32)
    for it in range(max(C, n_steps)):
        if it < C:
            o_intra = intra(it, o_intra)
        if it < n_steps:
            recurrence(it)
    for p in range(GLA_HEADS // 2):
        st_ref[p] = st[p]
    oi_ref[...] += o_intra

    for h in range(GLA_HEADS):
        sl = slice(h * hv, (h + 1) * hv)
        oh = oi_ref[:, sl]
        oh = oh * lax.rsqrt(jnp.mean(oh * oh, axis=-1, keepdims=True) + EPS) * og_ref[:, sl]
        rr = r_ref[:, sl]
        o_ref[:, sl] = (oh * (rr * jax.nn.sigmoid(rr))).astype(o_ref.dtype)


def _glamix(z3, wa_pad, ba, og):
    B, S, _ = z3.shape
    tg = min(256, S)
    blk = lambda w, c: pl.BlockSpec((None, tg, w), lambda b, i, c=c: (b, i, c))
    vec = lambda w: pl.BlockSpec((1, w), lambda b, i: (0, 0))
    return pl.pallas_call(
        functools.partial(_gla_body, tg=tg),
        grid=(B, S // tg),
        in_specs=[blk(GLA_DK, Z_GQ // GLA_DK), blk(GLA_DK, Z_GK // GLA_DK),
                  blk(GLA_DV, Z_GV // GLA_DV), blk(GLA_DV, Z_GR // GLA_DV),
                  blk(LANES, Z_SM // LANES),
                  pl.BlockSpec((LANES, GLA_DK), lambda b, i: (0, 0)), vec(GLA_DK), vec(GLA_DV)],
        out_specs=blk(GLA_DV, 0),
        out_shape=jax.ShapeDtypeStruct((B, S, GLA_DV), MXU_DT),
        scratch_shapes=[pltpu.VMEM((GLA_HEADS // 2, GLA_DV // GLA_HEADS, LANES), f32),
                        pltpu.VMEM((GLA_CHUNK + tg, GLA_DK), f32),
                        pltpu.VMEM((GLA_CHUNK + tg, GLA_DK), f32),
                        pltpu.VMEM((GLA_CHUNK + tg, GLA_DV), f32),
                        pltpu.VMEM((tg, GLA_DK), MXU_DT),
                        pltpu.VMEM((tg, GLA_DK), MXU_DT),
                        pltpu.VMEM((tg, GLA_DK), MXU_DT),
                        pltpu.VMEM((tg, GLA_DV), MXU_DT),
                        pltpu.VMEM((tg, GLA_DK), f32),
                        pltpu.VMEM((tg, GLA_DV), f32)],
        compiler_params=_cparams(("parallel", "arbitrary")),
        name="glamix",
    )(z3, z3, z3, z3, z3, wa_pad, ba, og)


def _merge_body(h_ref, oa_ref, ob_ref, oc_ref, od_ref, wg0, wg1, wg2, wg3, gb0, gb1, gb2, gb3,
                p0, p1, p2, p3, y_ref):
    h = h_ref[...]
    acc = None
    for o_ref, wg, gb, p in ((oa_ref, wg0, gb0, p0), (ob_ref, wg1, gb1, p1),
                             (oc_ref, wg2, gb2, p2), (od_ref, wg3, gb3, p3)):
        gate = jax.nn.sigmoid(jnp.dot(h, wg[...], preferred_element_type=f32) + gb[...])
        term = gate * jnp.dot(o_ref[...], p[...], preferred_element_type=f32)
        acc = term if acc is None else acc + term
    y_ref[...] = acc.astype(y_ref.dtype)


def _merge(h, outs, w_gate, gate_b, w_branch, l):
    T, D = h.shape
    tm, tn = min(1024, T), 512
    nj = D // tn
    g0 = Z_COLS // tn
    tok = lambda w: pl.BlockSpec((tm, w), lambda i, j: (i, 0))
    wg = [pl.BlockSpec((None, D, tn), lambda i, j, b=b: (l, 0, g0 + b * nj + j)) for b in range(4)]
    gb = [pl.BlockSpec((1, tn), lambda i, j, b=b: (0, b * nj + j)) for b in range(4)]
    pb = [pl.BlockSpec((None, None, BRANCH_W, tn), lambda i, j, b=b: (l, b, 0, j)) for b in range(4)]
    return pl.pallas_call(
        _merge_body,
        grid=(T // tm, nj),
        in_specs=[tok(D)] + [tok(BRANCH_W)] * 4 + wg + gb + pb,
        out_specs=pl.BlockSpec((tm, tn), lambda i, j: (i, j)),
        out_shape=jax.ShapeDtypeStruct((T, D), MXU_DT),
        compiler_params=_cparams(("parallel", "arbitrary")),
        name="merge",
    )(h, *outs, *([w_gate] * 4), *([gate_b] * 4), *([w_branch] * 4))


def _outproj_body(y_ref, x_ref, w_ref, g_ref, x1_ref, h2_ref):
    x1 = x_ref[...] + jnp.dot(y_ref[...], w_ref[...], preferred_element_type=f32)
    x1_ref[...] = x1
    ms = jnp.mean(x1 * x1, axis=-1, keepdims=True)
    h2_ref[...] = (x1 * lax.rsqrt(ms + EPS) * g_ref[...]).astype(h2_ref.dtype)


def _outproj(y, x2, w_out, g2, l):
    T, D = x2.shape
    tm = min(512, T)
    tok = pl.BlockSpec((tm, D), lambda i: (i, 0))
    return pl.pallas_call(
        _outproj_body,
        grid=(T // tm,),
        in_specs=[tok, tok, pl.BlockSpec((None, D, D), lambda i: (l, 0, 0)),
                  pl.BlockSpec((1, D), lambda i: (0, 0))],
        out_specs=[tok, tok],
        out_shape=[jax.ShapeDtypeStruct((T, D), f32), jax.ShapeDtypeStruct((T, D), MXU_DT)],
        compiler_params=_cparams(("parallel",)),
        name="outproj",
    )(y, x2, w_out, g2)


FFN_HALO = 8


def _ffnup_body(h_ref, wa_ref, wv_ref, dwa_ref, dwv_ref, dba_ref, dbv_ref, g_ref, ua_ref, uv_ref,
                *, tm, tiles_per_seq):
    @pl.when(pl.program_id(1) % tiles_per_seq == 0)
    def _():
        ua_ref[0:FFN_HALO, :] = jnp.zeros((FFN_HALO, ua_ref.shape[1]), f32)
        uv_ref[0:FFN_HALO, :] = jnp.zeros((FFN_HALO, uv_ref.shape[1]), f32)

    h = h_ref[...]

    def conv(w_ref, dw_ref, db_ref, u_ref):
        u_ref[FFN_HALO:FFN_HALO + tm, :] = jnp.dot(h, w_ref[...], preferred_element_type=f32)
        y = db_ref[...] + dw_ref[FFN_K - 1:FFN_K, :] * u_ref[FFN_HALO:FFN_HALO + tm, :]
        for k in range(FFN_K - 1):
            off = FFN_HALO - (FFN_K - 1) + k
            y = y + dw_ref[k:k + 1, :] * u_ref[off:off + tm, :]
        u_ref[0:FFN_HALO, :] = u_ref[tm:tm + FFN_HALO, :]
        return y

    a = conv(wa_ref, dwa_ref, dba_ref, ua_ref)
    v = conv(wv_ref, dwv_ref, dbv_ref, uv_ref)
    g_ref[...] = (a * jax.nn.sigmoid(a) * v).astype(g_ref.dtype)


def _ffnup(h2, ffn_up, ffn_dw, ffn_db, seq_len, l):
    T, D = h2.shape
    dff = ffn_up.shape[2] // 2
    tm, tn = min(1024, seq_len), 512
    nj = dff // tn
    return pl.pallas_call(
        functools.partial(_ffnup_body, tm=tm, tiles_per_seq=seq_len // tm),
        grid=(nj, T // tm),
        in_specs=[pl.BlockSpec((tm, D), lambda j, i: (i, 0)),
                  pl.BlockSpec((None, D, tn), lambda j, i: (l, 0, j)),
                  pl.BlockSpec((None, D, tn), lambda j, i: (l, 0, nj + j)),
                  pl.BlockSpec((FFN_K, tn), lambda j, i: (0, j)),
                  pl.BlockSpec((FFN_K, tn), lambda j, i: (0, nj + j)),
                  pl.BlockSpec((1, tn), lambda j, i: (0, j)),
                  pl.BlockSpec((1, tn), lambda j, i: (0, nj + j))],
        out_specs=pl.BlockSpec((tm, tn), lambda j, i: (i, j)),
        out_shape=jax.ShapeDtypeStruct((T, dff), MXU_DT),
        scratch_shapes=[pltpu.VMEM((FFN_HALO + tm, tn), f32), pltpu.VMEM((FFN_HALO + tm, tn), f32)],
        compiler_params=_cparams(("parallel", "arbitrary")),
        name="ffnup",
    )(h2, ffn_up, ffn_up, ffn_dw, ffn_dw, ffn_db, ffn_db)


def _ffndown_body(g_ref, w_ref, x_ref, o_ref):
    o_ref[...] = x_ref[...] + jnp.dot(g_ref[...], w_ref[...], preferred_element_type=f32)


def _ffndown(g, ffn_down, x1, l):
    T, D = x1.shape
    dff = g.shape[1]
    tm = min(512, T)
    return pl.pallas_call(
        _ffndown_body,
        grid=(T // tm,),
        in_specs=[pl.BlockSpec((tm, dff), lambda i: (i, 0)),
                  pl.BlockSpec((None, dff, D), lambda i: (l, 0, 0), pipeline_mode=pl.Buffered(1)),
                  pl.BlockSpec((tm, D), lambda i: (i, 0))],
        out_specs=pl.BlockSpec((tm, D), lambda i: (i, 0)),
        out_shape=jax.ShapeDtypeStruct((T, D), f32),
        compiler_params=_cparams(("parallel",)),
        name="ffndown",
    )(g, ffn_down, x1)


def _mixers(z3, p):
    B, S, _ = z3.shape
    qt, kt, vt = _foxprep(z3, p["fox_fb"], p["fox_qg"], p["fox_kg"])
    o_a = _fox_attention(qt, kt, vt)
    o_b = _convmix(z3, p["conv_dw"], p["conv_db"], p["conv_ln_g"], p["conv_ln_b"])
    o_c = _glamix(z3, p["gla_wa"], p["gla_ba"], p["gla_og"])
    o_d = _poolmix(z3, p["pool_w"], p["pool_scale"])
    return o_a, o_b, o_c, o_d


STACKED = ("w_cat", "w_branch", "w_out", "ffn_up", "ffn_down")


def _layer(x, params, l):
    B, S, D = x.shape
    T = B * S
    x2 = x.reshape(T, D)
    p = {k: (v if k in STACKED else v[l]) for k, v in params.items()}
    h, z = _inproj(x2, p["norm1_g"], p["w_cat"], l)
    outs = _mixers(z.reshape(B, S, Z_USED), p)
    y = _merge(h, [o.reshape(T, BRANCH_W) for o in outs], p["w_cat"], p["gate_b"], p["w_branch"], l)
    x1, h2 = _outproj(y, x2, p["w_out"], p["norm2_g"], l)
    g = _ffnup(h2, p["ffn_up"], p["ffn_dw"], p["ffn_db"], S, l)
    return _ffndown(g, p["ffn_down"], x1, l).reshape(B, S, D)


def _w_in_pieces(d_model):
    n_small = 3 * BRANCH_W + FOX_HEADS + 2 * BRANCH_W + 2 * GLA_DK + GLA_DV + GLA_RANK + GLA_DV + BRANCH_W
    o_ff = 3 * BRANCH_W
    o_cz = o_ff + FOX_HEADS
    o_ga = o_cz + 2 * BRANCH_W + 2 * GLA_DK + GLA_DV
    o_gr = o_ga + GLA_RANK
    return n_small, [(0, o_ff, Z_FQ), (o_cz, o_ga, Z_CA), (o_gr, n_small, Z_GR),
                     (o_ff, o_cz, Z_SM + SM_FF), (o_ga, o_gr, Z_SM + SM_GA),
                     (n_small, n_small + 4 * d_model, Z_COLS)]


RELAYOUT_CHUNK = 512


def _relayout_body(w_ref, o_ref, *, pieces):
    td = o_ref.shape[0]
    narrow = []
    for src0, src1, dst in pieces:
        if (src1 - src0) % RELAYOUT_CHUNK:
            narrow.append((src0, src1, dst))
            continue
        for c in range(0, src1 - src0, RELAYOUT_CHUNK):
            o_ref[:, dst + c:dst + c + RELAYOUT_CHUNK] = (
                w_ref[src0 + c:src0 + c + RELAYOUT_CHUNK, :].T.astype(o_ref.dtype))
    assert narrow[0][2] == Z_SM and all(a[2] + a[1] - a[0] == b[2] for a, b in zip(narrow, narrow[1:]))
    used = sum(s1 - s0 for s0, s1, _ in narrow)
    rows = [w_ref[s0:s1, :] for s0, s1, _ in narrow] + [jnp.zeros((LANES - used, td), f32)]
    o_ref[:, Z_SM:Z_SM + LANES] = jnp.concatenate(rows, axis=0).T.astype(o_ref.dtype)
    o_ref[:, Z_SM + LANES:Z_COLS] = jnp.zeros((td, Z_COLS - Z_SM - LANES), o_ref.dtype)


def _relayout_w_in(w_in):
    L, D, n_in = w_in.shape
    _, pieces = _w_in_pieces(D)
    n_out = Z_COLS + 4 * D
    td = 256
    return pl.pallas_call(
        functools.partial(_relayout_body, pieces=pieces),
        grid=(L, D // td),
        in_specs=[pl.BlockSpec((None, n_in, td), lambda l, i: (l, 0, i))],
        out_specs=pl.BlockSpec((None, td, n_out), lambda l, i: (l, i, 0)),
        out_shape=jax.ShapeDtypeStruct((L, D, n_out), MXU_DT),
        compiler_params=_cparams(("parallel", "parallel")),
        name="relayout_w_in",
    )(jnp.swapaxes(w_in, 1, 2))


def _prepare(norm1_g, w_in, fox_fb, fox_qg, fox_kg, conv_dw, conv_db, conv_ln_g, conv_ln_b,
             gla_wa, gla_ba, gla_og, pool_w, pool_scale, gate_b, w_branch, w_out,
             norm2_g, ffn_up, ffn_dw, ffn_db, ffn_down):
    w_cat = _relayout_w_in(w_in)
    row = lambda a: a[:, None, :]
    pad_lanes = lambda a, off: jnp.pad(a, ((0, 0), (off, LANES - off - a.shape[1])))
    wa_pad = jnp.pad(gla_wa, ((0, 0), (SM_GA, LANES - SM_GA - GLA_RANK), (0, 0)))
    return {
        "norm1_g": row(norm1_g),
        "w_cat": w_cat,
        "fox_fb": row(pad_lanes(fox_fb, SM_FF)),
        "fox_qg": row(jnp.tile(fox_qg, (1, FOX_HEADS))),
        "fox_kg": row(jnp.tile(fox_kg, (1, FOX_HEADS))),
        "conv_dw": jnp.pad(conv_dw, ((0, 0), (0, 1), (0, 0))),
        "conv_db": row(conv_db), "conv_ln_g": row(conv_ln_g), "conv_ln_b": row(conv_ln_b),
        "gla_wa": wa_pad.astype(MXU_DT), "gla_ba": row(gla_ba), "gla_og": row(gla_og),
        "pool_w": pool_w.astype(MXU_DT), "pool_scale": row(pool_scale),
        "gate_b": row(gate_b),
        "w_branch": w_branch.astype(MXU_DT),
        "w_out": w_out.astype(MXU_DT),
        "norm2_g": row(norm2_g),
        "ffn_up": ffn_up.astype(MXU_DT), "ffn_dw": ffn_dw, "ffn_db": row(ffn_db),
        "ffn_down": ffn_down.astype(MXU_DT),
    }


def kernel(x, norm1_g, w_in, fox_fb, fox_qg, fox_kg, conv_dw, conv_db, conv_ln_g, conv_ln_b, gla_wa, gla_ba, gla_og, pool_w, pool_scale, gate_b, w_branch, w_out, norm2_g, ffn_up, ffn_dw, ffn_db, ffn_down):
    params = _prepare(norm1_g, w_in, fox_fb, fox_qg, fox_kg, conv_dw, conv_db, conv_ln_g, conv_ln_b,
                      gla_wa, gla_ba, gla_og, pool_w, pool_scale, gate_b, w_branch, w_out,
                      norm2_g, ffn_up, ffn_dw, ffn_db, ffn_down)
    for l in range(w_in.shape[0]):
        x = _layer(x, params, l)
    return x
```

```python
import functools

import jax
import jax.numpy as jnp
import numpy as np
from jax import lax
from jax.experimental import pallas as pl
from jax.experimental.pallas import tpu as pltpu

f32 = jnp.float32
MXU_DT = jnp.bfloat16
HI = lax.Precision.HIGHEST

EPS = 1e-6
BRANCH_W = 512
FOX_HEADS = 8
FOX_HD = 64
CONV_K = 31
GLA_HEADS = 4
GLA_DK = 256
GLA_DV = 512
GLA_RANK = 16
GLA_TEMP = 16.0
GLA_CHUNK = 16
POOL_WINDOWS = (2, 4, 8, 16)
FFN_K = 3
LANES = 128

Z_FQ, Z_FK, Z_FV, Z_CA, Z_CG = 0, 512, 1024, 1536, 2048
Z_GQ, Z_GK, Z_GV, Z_GR, Z_PZ, Z_SM, Z_COLS = 2560, 2816, 3072, 3584, 4096, 4608, 5120
Z_USED = Z_SM + 128
SM_FF, SM_GA = 0, 8

NEG_BIG = -1e30
LOG2E = 1.4426950408889634
VMEM_LIMIT = 56 * 1024 * 1024


def _cparams(sem):
    return pltpu.CompilerParams(dimension_semantics=sem, vmem_limit_bytes=VMEM_LIMIT)


def _group(idx, size):
    assert size & (size - 1) == 0
    return idx >> (size.bit_length() - 1)


def _log_sigmoid(x):
    return jnp.minimum(x, 0.0) - jnp.log1p(jnp.exp(-jnp.abs(x)))


def _inproj_body(x_ref, g_ref, w_ref, h_ref, z_ref):
    x = x_ref[...]
    ms = jnp.mean(x * x, axis=-1, keepdims=True)
    h = (x * lax.rsqrt(ms + EPS) * g_ref[...]).astype(h_ref.dtype)
    h_ref[...] = h
    z_ref[...] = jnp.dot(h, w_ref[...], preferred_element_type=f32)


def _inproj(x2, g, w_cat, l):
    T, D = x2.shape
    tm = min(512, T)
    return pl.pallas_call(
        _inproj_body,
        grid=(T // tm,),
        in_specs=[pl.BlockSpec((tm, D), lambda i: (i, 0)),
                  pl.BlockSpec((1, D), lambda i: (0, 0)),
                  pl.BlockSpec((None, D, Z_USED), lambda i: (l, 0, 0), pipeline_mode=pl.Buffered(1))],
        out_specs=[pl.BlockSpec((tm, D), lambda i: (i, 0)),
                   pl.BlockSpec((tm, Z_USED), lambda i: (i, 0))],
        out_shape=[jax.ShapeDtypeStruct((T, D), MXU_DT),
                   jax.ShapeDtypeStruct((T, Z_USED), f32)],
        compiler_params=_cparams(("parallel",)),
        name="inproj",
    )(x2, g, w_cat)


def _head_rmsnorm(x, g, lo):
    sq = x * x
    s0 = jnp.sum(jnp.where(lo, sq, 0.0), axis=-1, keepdims=True)
    s1 = jnp.sum(jnp.where(lo, 0.0, sq), axis=-1, keepdims=True)
    r = jnp.where(lo, lax.rsqrt(s0 * (1.0 / FOX_HD) + EPS), lax.rsqrt(s1 * (1.0 / FOX_HD) + EPS))
    return x * r * g


N_SPLIT = 3
L_KC = FOX_HD
L_QC = FOX_HD + N_SPLIT
L_ONE = FOX_HD
V_ROWS = FOX_HD + 16


def _split_pieces(x):
    pieces, rest = [], x
    for _ in range(N_SPLIT):
        piece = rest.astype(jnp.bfloat16)
        pieces.append(piece)
        rest = rest - piece.astype(f32)
    return pieces


def _foxprep_body(q_ref, k_ref, v_ref, sm_ref, fb_ref, qg_ref, kg_ref,
                  qt_ref, kt_ref, vt_ref, carry_ref, *, tp):
    i = pl.program_id(1)

    @pl.when(i == 0)
    def _():
        carry_ref[...] = jnp.zeros_like(carry_ref)

    logf = _log_sigmoid(sm_ref[...] + fb_ref[...])
    row = lax.broadcasted_iota(jnp.int32, (tp, tp), 0)
    col = lax.broadcasted_iota(jnp.int32, (tp, tp), 1)
    tri = (col <= row).astype(f32)
    c = jnp.dot(tri, logf, preferred_element_type=f32, precision=HI) + carry_ref[0:1, :]
    carry_ref[0:1, :] = c[tp - 1:tp, :]

    lane = lax.broadcasted_iota(jnp.int32, (1, LANES), 1)
    lo = lane < FOX_HD
    ones_q = ((lane >= L_KC) & (lane < L_KC + N_SPLIT)).astype(f32)
    ones_k = ((lane >= L_QC) & (lane < L_QC + N_SPLIT)).astype(f32)
    ones_v = (lane == L_ONE).astype(f32)
    in_qc = (lane >= L_QC) & (lane < L_QC + N_SPLIT)
    in_kc = (lane >= L_KC) & (lane < L_KC + N_SPLIT)
    pieces = [p.astype(f32) for p in _split_pieces(c * LOG2E)]
    scale = FOX_HD ** -0.5 * LOG2E
    for pair in range(BRANCH_W // LANES):
        sl = slice(pair * LANES, (pair + 1) * LANES)
        qn = _head_rmsnorm(q_ref[:, sl], qg_ref[:, sl], lo) * scale
        kn = _head_rmsnorm(k_ref[:, sl], kg_ref[:, sl], lo)
        vv = v_ref[:, sl]
        for e in (0, 1):
            h = 2 * pair + e
            if e == 1:
                qn, kn, vv = (pltpu.roll(t, FOX_HD, axis=1) for t in (qn, kn, vv))
            cp = jnp.zeros((tp, LANES), f32)
            for j, piece in enumerate(pieces):
                pj = jnp.broadcast_to(piece[:, h:h + 1], (tp, LANES))
                cp = jnp.where((lane == L_KC + j) | (lane == L_QC + j), pj, cp)
            qt_ref[h] = jnp.where(lo, qn, jnp.where(in_qc, cp, ones_q)).astype(qt_ref.dtype)
            kt_ref[h] = jnp.where(lo, kn, jnp.where(in_kc, -cp, ones_k)).astype(kt_ref.dtype)
            vt_ref[h] = jnp.where(lo, vv, ones_v).T[0:V_ROWS, :].astype(vt_ref.dtype)


def _foxprep(z3, fb_pad, qg, kg):
    B, S, _ = z3.shape
    tp = min(512, S)
    blk = lambda w, c: pl.BlockSpec((None, tp, w), lambda b, i, c=c: (b, i, c))
    vec = lambda w: pl.BlockSpec((1, w), lambda b, i: (0, 0))
    head_blk = pl.BlockSpec((None, FOX_HEADS, tp, LANES), lambda b, i: (b, 0, i, 0))
    return pl.pallas_call(
        functools.partial(_foxprep_body, tp=tp),
        grid=(B, S // tp),
        in_specs=[blk(BRANCH_W, Z_FQ // BRANCH_W), blk(BRANCH_W, Z_FK // BRANCH_W),
                  blk(BRANCH_W, Z_FV // BRANCH_W), blk(LANES, Z_SM // LANES),
                  vec(LANES), vec(BRANCH_W), vec(BRANCH_W)],
        out_specs=[head_blk, head_blk,
                   pl.BlockSpec((None, FOX_HEADS, V_ROWS, tp), lambda b, i: (b, 0, 0, i))],
        out_shape=[jax.ShapeDtypeStruct((B, FOX_HEADS, S, LANES), MXU_DT)] * 2
        + [jax.ShapeDtypeStruct((B, FOX_HEADS, V_ROWS, S), MXU_DT)],
        scratch_shapes=[pltpu.VMEM((8, LANES), f32)],
        compiler_params=_cparams(("parallel", "arbitrary")),
        name="foxprep",
    )(z3, z3, z3, z3, fb_pad, qg, kg)


def _fox_body(q_ref, k_ref, v_ref, o_ref, m_ref, acc_ref, *, tq):
    i = pl.program_id(2)
    m_ref[...] = jnp.full_like(m_ref, NEG_BIG)
    acc_ref[...] = jnp.zeros_like(acc_ref)

    heads = range(FOX_GROUP)

    def blocks(j, masked):
        start = pl.multiple_of(j * tq, tq)
        s = [lax.dot_general(k_ref[g, pl.ds(start, tq), :], q_ref[g], (((1,), (1,)), ((), ())),
                             preferred_element_type=f32) for g in heads]
        if masked:
            key = lax.broadcasted_iota(jnp.int32, (tq, tq), 0)
            qry = lax.broadcasted_iota(jnp.int32, (tq, tq), 1)
            s = [jnp.where(key <= qry, sg, NEG_BIG) for sg in s]
        m_prev = [m_ref[g, 0:1, :] for g in heads]
        m_new = [jnp.maximum(m_prev[g], jnp.max(s[g], axis=0, keepdims=True)) for g in heads]
        p = [jnp.exp2(s[g] - m_new[g]).astype(q_ref.dtype) for g in heads]
        pv = [jnp.dot(v_ref[g, :, pl.ds(start, tq)], p[g], preferred_element_type=f32) for g in heads]
        for g in heads:
            m_ref[g, 0:1, :] = m_new[g]
            acc_ref[g] = jnp.exp2(m_prev[g] - m_new[g]) * acc_ref[g] + pv[g]

    def full_blocks(jj, carry):
        for u in range(FOX_UNROLL):
            blocks(FOX_UNROLL * jj + u, False)
        return carry

    lax.fori_loop(0, i // FOX_UNROLL, full_blocks, 0)
    for u in range(FOX_UNROLL - 1):
        @pl.when(i % FOX_UNROLL > u)
        def _():
            blocks(i - i % FOX_UNROLL + u, False)

    blocks(i, True)
    for pair in range(FOX_GROUP // 2):
        o_t = jnp.concatenate(
            [acc_ref[g, 0:FOX_HD, :] / acc_ref[g, L_ONE:L_ONE + 1, :] for g in (2 * pair, 2 * pair + 1)], axis=0)
        o_ref[:, pair * LANES:(pair + 1) * LANES] = o_t.T.astype(o_ref.dtype)


FOX_GROUP = 8
FOX_UNROLL = 4


def _fox_attention(qt, kt, vtt):
    B, H, S, _ = qt.shape
    tq = min(256, S)
    G = FOX_GROUP
    assert G == H
    resident = dict(pipeline_mode=pl.Buffered(1))
    return pl.pallas_call(
        functools.partial(_fox_body, tq=tq),
        grid=(B, H // G, S // tq),
        in_specs=[pl.BlockSpec((None, G, tq, LANES), lambda b, h, i: (b, h, i, 0)),
                  pl.BlockSpec((None, G, S, LANES), lambda b, h, i: (b, h, 0, 0), **resident),
                  pl.BlockSpec((None, G, V_ROWS, S), lambda b, h, i: (b, h, 0, 0), **resident)],
        out_specs=pl.BlockSpec((None, tq, H * FOX_HD), lambda b, h, i: (b, i, 0)),
        out_shape=jax.ShapeDtypeStruct((B, S, H * FOX_HD), MXU_DT),
        scratch_shapes=[pltpu.VMEM((G, 8, tq), f32), pltpu.VMEM((G, V_ROWS, tq), f32)],
        compiler_params=_cparams(("parallel", "parallel", "arbitrary")),
        name="fox_attn",
    )(qt, kt, vtt)


CONV_HALO = 32
CONV_ROWS = 64


def _convmix_body(a_ref, g_ref, dw_ref, db_ref, lng_ref, lnb_ref, o_ref, u_ref, us_ref, *, tp):
    @pl.when(pl.program_id(1) == 0)
    def _():
        u_ref[0:CONV_HALO, :] = jnp.zeros((CONV_HALO, BRANCH_W), f32)

    u_ref[CONV_HALO:CONV_HALO + tp, :] = a_ref[...] * jax.nn.sigmoid(g_ref[...])
    n_shift = CONV_HALO + tp - 8
    for rho in range(1, 8):
        us_ref[rho - 1, 0:n_shift, :] = u_ref[rho:rho + n_shift, :]
    for r in range(0, tp, CONV_ROWS):
        acc = jnp.broadcast_to(db_ref[...], (CONV_ROWS, BRANCH_W))
        for k in range(CONV_K):
            q8, rho = divmod(CONV_HALO - (CONV_K - 1) + k, 8)
            off = r + 8 * q8
            tap = u_ref[off:off + CONV_ROWS, :] if rho == 0 else us_ref[rho - 1, off:off + CONV_ROWS, :]
            acc = acc + dw_ref[k:k + 1, :] * tap
        mu = jnp.mean(acc, axis=-1, keepdims=True)
        d = acc - mu
        var = jnp.mean(d * d, axis=-1, keepdims=True)
        y = d * lax.rsqrt(var + EPS) * lng_ref[...] + lnb_ref[...]
        o_ref[r:r + CONV_ROWS, :] = (y * jax.nn.sigmoid(y)).astype(o_ref.dtype)
    u_ref[0:CONV_HALO, :] = u_ref[tp:tp + CONV_HALO, :]


def _convmix(z3, dw_pad, db, lng, lnb):
    B, S, _ = z3.shape
    tp = min(256, S)
    blk = lambda c: pl.BlockSpec((None, tp, BRANCH_W), lambda b, i, c=c: (b, i, c))
    vec = pl.BlockSpec((1, BRANCH_W), lambda b, i: (0, 0))
    return pl.pallas_call(
        functools.partial(_convmix_body, tp=tp),
        grid=(B, S // tp),
        in_specs=[blk(Z_CA // BRANCH_W), blk(Z_CG // BRANCH_W),
                  pl.BlockSpec(dw_pad.shape, lambda b, i: (0, 0)), vec, vec, vec],
        out_specs=blk(0),
        out_shape=jax.ShapeDtypeStruct((B, S, BRANCH_W), MXU_DT),
        scratch_shapes=[pltpu.VMEM((CONV_HALO + tp, BRANCH_W), f32),
                        pltpu.VMEM((7, CONV_HALO + tp, BRANCH_W), f32)],
        compiler_params=_cparams(("parallel", "arbitrary")),
        name="convmix",
    )(z3, z3, dw_pad, db, lng, lnb)


POOL_HALO = 16


def _poolmix_body(u_in_ref, pw_ref, sc_ref, o_ref, u_ref, *, tp):
    i = pl.program_id(1)

    @pl.when(i == 0)
    def _():
        u_ref[0:POOL_HALO, :] = jnp.zeros((POOL_HALO, BRANCH_W), f32)

    u_ref[POOL_HALO:POOL_HALO + tp, :] = u_in_ref[...]
    pos = i * tp + lax.broadcasted_iota(jnp.int32, (tp, 1), 0)
    for gi, w in enumerate(POOL_WINDOWS):
        sl = slice(gi * LANES, (gi + 1) * LANES)
        acc = u_ref[POOL_HALO:POOL_HALO + tp, sl]
        for j in range(1, w):
            acc = acc + u_ref[POOL_HALO - j:POOL_HALO - j + tp, sl]
        cnt = jnp.minimum(pos + 1, w).astype(f32)
        mixed = acc / cnt - u_ref[POOL_HALO:POOL_HALO + tp, sl]
        out = jnp.dot(mixed.astype(pw_ref.dtype), pw_ref[gi], preferred_element_type=f32)
        o_ref[:, sl] = (out * sc_ref[:, sl]).astype(o_ref.dtype)
    u_ref[0:POOL_HALO, :] = u_ref[tp:tp + POOL_HALO, :]


def _poolmix(z3, pw, scale):
    B, S, _ = z3.shape
    tp = min(512, S)
    return pl.pallas_call(
        functools.partial(_poolmix_body, tp=tp),
        grid=(B, S // tp),
        in_specs=[pl.BlockSpec((None, tp, BRANCH_W), lambda b, i: (b, i, Z_PZ // BRANCH_W)),
                  pl.BlockSpec(pw.shape, lambda b, i: (0, 0, 0)),
                  pl.BlockSpec((1, BRANCH_W), lambda b, i: (0, 0))],
        out_specs=pl.BlockSpec((None, tp, BRANCH_W), lambda b, i: (b, i, 0)),
        out_shape=jax.ShapeDtypeStruct((B, S, BRANCH_W), MXU_DT),
        scratch_shapes=[pltpu.VMEM((POOL_HALO + tp, BRANCH_W), f32)],
        compiler_params=_cparams(("parallel", "arbitrary")),
        name="poolmix",
    )(z3, pw, scale)


def _gla_body(q_ref, k_ref, v_ref, r_ref, sm_ref, wa_ref, ba_ref, og_ref, sel_ref, o_ref,
              st_ref, kb_ref, bb_ref, qlo_ref, qhi_ref, kd_ref, vh_ref, dec_ref, oi_ref, up_ref, sb_ref,
              *, tg):
    C = GLA_CHUNK
    hk = GLA_DK // GLA_HEADS
    hv = GLA_DV // GLA_HEADS

    @pl.when(pl.program_id(1) == 0)
    def _():
        st_ref[...] = jnp.zeros_like(st_ref)

    pre = jnp.dot(sm_ref[...].astype(wa_ref.dtype), wa_ref[...], preferred_element_type=f32) + ba_ref[...]
    loga = _log_sigmoid(pre) * (LOG2E / GLA_TEMP)
    row = lax.broadcasted_iota(jnp.int32, (tg, tg), 0)
    col = lax.broadcasted_iota(jnp.int32, (tg, tg), 1)
    same = _group(row, C) == _group(col, C)
    bc = jnp.dot((same & (col <= row)).astype(f32), loga, preferred_element_type=f32, precision=HI)
    blast = jnp.dot(same.astype(f32), loga, preferred_element_type=f32, precision=HI)

    q = q_ref[...] * (hk ** -0.5)
    k = k_ref[...]
    qd = q * jnp.exp2(bc)
    first = (lax.broadcasted_iota(jnp.int32, (1, GLA_DK), 1) & (LANES - 1)) < hk
    qlo_ref[...] = jnp.where(first, qd, 0.0).astype(qlo_ref.dtype)
    qhi_ref[...] = jnp.where(first, 0.0, qd).astype(qhi_ref.dtype)
    kd_ref[...] = (k * jnp.exp2(blast - bc)).astype(kd_ref.dtype)
    dec_ref[...] = jnp.exp2(blast)
    vh_ref[...] = v_ref[...].astype(vh_ref.dtype)

    kb_ref[0:C, :] = jnp.zeros((C, GLA_DK), f32)
    bb_ref[0:C, :] = jnp.zeros((C, GLA_DK), f32)
    kb_ref[C:C + tg, :] = k
    bb_ref[C:C + tg, :] = bc
    rpos = lax.broadcasted_iota(jnp.int32, (tg, 1), 0) & (C - 1)
    lo = lax.broadcasted_iota(jnp.int32, (1, LANES), 1) < hk
    st = [st_ref[p] for p in range(GLA_HEADS // 2)]

    def intra(delta, acc):
        valid = rpos >= delta
        ks = kb_ref[C - delta:C - delta + tg, :]
        bs = bb_ref[C - delta:C - delta + tg, :]
        w = jnp.where(valid, q * ks * jnp.exp2(jnp.where(valid, bc - bs, 0.0)), 0.0)
        return acc + jnp.dot(w.astype(MXU_DT), sel_ref[delta], preferred_element_type=f32)

    n_steps = tg // C
    pairs = range(GLA_HEADS // 2)

    def increment(n):
        rows = slice(n * C, (n + 1) * C)
        for p in pairs:
            ps = slice(p * LANES, (p + 1) * LANES)
            upd = [lax.dot_general(vh_ref[rows, (2 * p + e) * hv:(2 * p + e + 1) * hv], kd_ref[rows, ps],
                                   (((0,), (0,)), ((), ())), preferred_element_type=f32) for e in (0, 1)]
            up_ref[n, p] = jnp.where(lo, upd[0], upd[1])

    def read_out(n):
        rows = slice(n * C, (n + 1) * C)
        for p in pairs:
            ps = slice(p * LANES, (p + 1) * LANES)
            lhs = jnp.concatenate([qlo_ref[rows, ps], qhi_ref[rows, ps]], axis=0)
            o_p = lax.dot_general(lhs, sb_ref[n, p], (((1,), (1,)), ((), ())),
                                  preferred_element_type=f32)
            for e in (0, 1):
                oi_ref[rows, (2 * p + e) * hv:(2 * p + e + 1) * hv] = o_p[e * C:(e + 1) * C]

    per_it = 2 * n_steps // C
    attn = jnp.zeros((tg, GLA_DK), f32)
    for it in range(C // 2):
        attn = intra(it, attn)
        for n in range(it * per_it, min((it + 1) * per_it, n_steps)):
            increment(n)
    for n in range(n_steps):
        for p in pairs:
            sb_ref[n, p] = st[p].astype(sb_ref.dtype)
            st[p] = st[p] * dec_ref[n * C:n * C + 1, p * LANES:(p + 1) * LANES] + up_ref[n, p]
    for p in pairs:
        st_ref[p] = st[p]
    for it in range(C // 2, C):
        attn = intra(it, attn)
        for n in range((it - C // 2) * per_it, min((it - C // 2 + 1) * per_it, n_steps)):
            read_out(n)

    assert tg == GLA_DK
    lane_t = lax.broadcasted_iota(jnp.int32, (1, tg), 1)
    for h in range(GLA_HEADS):
        mine = attn if h == 0 else pltpu.roll(attn, tg - h * hk, axis=1)
        band = pltpu.roll(jnp.where(lane_t < C, mine, 0.0), tg - (C - 1), axis=1, stride=1, stride_axis=0)
        vs = slice(h * hv, (h + 1) * hv)
        oi_ref[:, vs] += jnp.dot(band.astype(MXU_DT), vh_ref[:, vs], preferred_element_type=f32)

    for h in range(GLA_HEADS):
        sl = slice(h * hv, (h + 1) * hv)
        oh = oi_ref[:, sl]
        oh = oh * lax.rsqrt(jnp.mean(oh * oh, axis=-1, keepdims=True) + EPS) * og_ref[:, sl]
        rr = r_ref[:, sl]
        o_ref[:, sl] = (oh * (rr * jax.nn.sigmoid(rr))).astype(o_ref.dtype)


def _gla_selectors():
    hk = GLA_DK // GLA_HEADS
    lane = np.arange(GLA_DK)
    sel = np.zeros((GLA_CHUNK, GLA_DK, GLA_DK), np.float32)
    for delta in range(GLA_CHUNK):
        sel[delta, lane, (lane // hk) * hk + (GLA_CHUNK - 1 - delta)] = 1.0
    return jnp.asarray(sel, MXU_DT)


def _glamix(z3, wa_pad, ba, og):
    B, S, _ = z3.shape
    tg = min(256, S)
    blk = lambda w, c: pl.BlockSpec((None, tg, w), lambda b, i, c=c: (b, i, c))
    vec = lambda w: pl.BlockSpec((1, w), lambda b, i: (0, 0))
    return pl.pallas_call(
        functools.partial(_gla_body, tg=tg),
        grid=(B, S // tg),
        in_specs=[blk(GLA_DK, Z_GQ // GLA_DK), blk(GLA_DK, Z_GK // GLA_DK),
                  blk(GLA_DV, Z_GV // GLA_DV), blk(GLA_DV, Z_GR // GLA_DV),
                  blk(LANES, Z_SM // LANES),
                  pl.BlockSpec((LANES, GLA_DK), lambda b, i: (0, 0)), vec(GLA_DK), vec(GLA_DV),
                  pl.BlockSpec((GLA_CHUNK, GLA_DK, GLA_DK), lambda b, i: (0, 0, 0))],
        out_specs=blk(GLA_DV, 0),
        out_shape=jax.ShapeDtypeStruct((B, S, GLA_DV), MXU_DT),
        scratch_shapes=[pltpu.VMEM((GLA_HEADS // 2, GLA_DV // GLA_HEADS, LANES), f32),
                        pltpu.VMEM((GLA_CHUNK + tg, GLA_DK), f32),
                        pltpu.VMEM((GLA_CHUNK + tg, GLA_DK), f32),
                        pltpu.VMEM((tg, GLA_DK), MXU_DT),
                        pltpu.VMEM((tg, GLA_DK), MXU_DT),
                        pltpu.VMEM((tg, GLA_DK), MXU_DT),
                        pltpu.VMEM((tg, GLA_DV), MXU_DT),
                        pltpu.VMEM((tg, GLA_DK), f32),
                        pltpu.VMEM((tg, GLA_DV), f32),
                        pltpu.VMEM((tg // GLA_CHUNK, GLA_HEADS // 2, GLA_DV // GLA_HEADS, LANES), f32),
                        pltpu.VMEM((tg // GLA_CHUNK, GLA_HEADS // 2, GLA_DV // GLA_HEADS, LANES), MXU_DT)],
        compiler_params=_cparams(("parallel", "arbitrary")),
        name="glamix",
    )(z3, z3, z3, z3, z3, wa_pad, ba, og, _gla_selectors())


def _merge_body(h_ref, oa_ref, ob_ref, oc_ref, od_ref, wg0, wg1, wg2, wg3, gb0, gb1, gb2, gb3,
                p0, p1, p2, p3, y_ref):
    h = h_ref[...]
    acc = None
    for o_ref, wg, gb, p in ((oa_ref, wg0, gb0, p0), (ob_ref, wg1, gb1, p1),
                             (oc_ref, wg2, gb2, p2), (od_ref, wg3, gb3, p3)):
        gate = jax.nn.sigmoid(jnp.dot(h, wg[...], preferred_element_type=f32) + gb[...])
        term = gate * jnp.dot(o_ref[...], p[...], preferred_element_type=f32)
        acc = term if acc is None else acc + term
    y_ref[...] = acc.astype(y_ref.dtype)


def _merge(h, outs, w_gate, gate_b, w_branch, l):
    T, D = h.shape
    tm, tn = min(1024, T), 512
    nj = D // tn
    g0 = Z_COLS // tn
    tok = lambda w: pl.BlockSpec((tm, w), lambda i, j: (i, 0))
    wg = [pl.BlockSpec((None, D, tn), lambda i, j, b=b: (l, 0, g0 + b * nj + j)) for b in range(4)]
    gb = [pl.BlockSpec((1, tn), lambda i, j, b=b: (0, b * nj + j)) for b in range(4)]
    pb = [pl.BlockSpec((None, None, BRANCH_W, tn), lambda i, j, b=b: (l, b, 0, j)) for b in range(4)]
    return pl.pallas_call(
        _merge_body,
        grid=(T // tm, nj),
        in_specs=[tok(D)] + [tok(BRANCH_W)] * 4 + wg + gb + pb,
        out_specs=pl.BlockSpec((tm, tn), lambda i, j: (i, j)),
        out_shape=jax.ShapeDtypeStruct((T, D), MXU_DT),
        compiler_params=_cparams(("parallel", "arbitrary")),
        name="merge",
    )(h, *outs, *([w_gate] * 4), *([gate_b] * 4), *([w_branch] * 4))


def _outproj_body(y_ref, x_ref, w_ref, g_ref, x1_ref, h2_ref):
    x1 = x_ref[...] + jnp.dot(y_ref[...], w_ref[...], preferred_element_type=f32)
    x1_ref[...] = x1
    ms = jnp.mean(x1 * x1, axis=-1, keepdims=True)
    h2_ref[...] = (x1 * lax.rsqrt(ms + EPS) * g_ref[...]).astype(h2_ref.dtype)


def _outproj(y, x2, w_out, g2, l):
    T, D = x2.shape
    tm = min(512, T)
    tok = pl.BlockSpec((tm, D), lambda i: (i, 0))
    return pl.pallas_call(
        _outproj_body,
        grid=(T // tm,),
        in_specs=[tok, tok, pl.BlockSpec((None, D, D), lambda i: (l, 0, 0)),
                  pl.BlockSpec((1, D), lambda i: (0, 0))],
        out_specs=[tok, tok],
        out_shape=[jax.ShapeDtypeStruct((T, D), f32), jax.ShapeDtypeStruct((T, D), MXU_DT)],
        compiler_params=_cparams(("parallel",)),
        name="outproj",
    )(y, x2, w_out, g2)


FFN_HALO = 8


def _ffnup_body(h_ref, wa_ref, wv_ref, dwa_ref, dwv_ref, dba_ref, dbv_ref, g_ref, ua_ref, uv_ref,
                *, tm, tiles_per_seq):
    @pl.when(pl.program_id(1) % tiles_per_seq == 0)
    def _():
        ua_ref[0:FFN_HALO, :] = jnp.zeros((FFN_HALO, ua_ref.shape[1]), f32)
        uv_ref[0:FFN_HALO, :] = jnp.zeros((FFN_HALO, uv_ref.shape[1]), f32)

    h = h_ref[...]

    def conv(w_ref, dw_ref, db_ref, u_ref):
        u_ref[FFN_HALO:FFN_HALO + tm, :] = jnp.dot(h, w_ref[...], preferred_element_type=f32)
        y = db_ref[...] + dw_ref[FFN_K - 1:FFN_K, :] * u_ref[FFN_HALO:FFN_HALO + tm, :]
        for k in range(FFN_K - 1):
            off = FFN_HALO - (FFN_K - 1) + k
            y = y + dw_ref[k:k + 1, :] * u_ref[off:off + tm, :]
        u_ref[0:FFN_HALO, :] = u_ref[tm:tm + FFN_HALO, :]
        return y

    a = conv(wa_ref, dwa_ref, dba_ref, ua_ref)
    v = conv(wv_ref, dwv_ref, dbv_ref, uv_ref)
    g_ref[...] = (a * jax.nn.sigmoid(a) * v).astype(g_ref.dtype)


def _ffnup(h2, ffn_up, ffn_dw, ffn_db, seq_len, l):
    T, D = h2.shape
    dff = ffn_up.shape[2] // 2
    tm, tn = min(1024, seq_len), 512
    nj = dff // tn
    return pl.pallas_call(
        functools.partial(_ffnup_body, tm=tm, tiles_per_seq=seq_len // tm),
        grid=(nj, T // tm),
        in_specs=[pl.BlockSpec((tm, D), lambda j, i: (i, 0)),
                  pl.BlockSpec((None, D, tn), lambda j, i: (l, 0, j)),
                  pl.BlockSpec((None, D, tn), lambda j, i: (l, 0, nj + j)),
                  pl.BlockSpec((FFN_K, tn), lambda j, i: (0, j)),
                  pl.BlockSpec((FFN_K, tn), lambda j, i: (0, nj + j)),
                  pl.BlockSpec((1, tn), lambda j, i: (0, j)),
                  pl.BlockSpec((1, tn), lambda j, i: (0, nj + j))],
        out_specs=pl.BlockSpec((tm, tn), lambda j, i: (i, j)),
        out_shape=jax.ShapeDtypeStruct((T, dff), MXU_DT),
        scratch_shapes=[pltpu.VMEM((FFN_HALO + tm, tn), f32), pltpu.VMEM((FFN_HALO + tm, tn), f32)],
        compiler_params=_cparams(("parallel", "arbitrary")),
        name="ffnup",
    )(h2, ffn_up, ffn_up, ffn_dw, ffn_dw, ffn_db, ffn_db)


def _ffndown_body(g_ref, w_ref, x_ref, o_ref):
    o_ref[...] = x_ref[...] + jnp.dot(g_ref[...], w_ref[...], preferred_element_type=f32)


def _ffndown(g, ffn_down, x1, l):
    T, D = x1.shape
    dff = g.shape[1]
    tm = min(512, T)
    return pl.pallas_call(
        _ffndown_body,
        grid=(T // tm,),
        in_specs=[pl.BlockSpec((tm, dff), lambda i: (i, 0)),
                  pl.BlockSpec((None, dff, D), lambda i: (l, 0, 0), pipeline_mode=pl.Buffered(1)),
                  pl.BlockSpec((tm, D), lambda i: (i, 0))],
        out_specs=pl.BlockSpec((tm, D), lambda i: (i, 0)),
        out_shape=jax.ShapeDtypeStruct((T, D), f32),
        compiler_params=_cparams(("parallel",)),
        name="ffndown",
    )(g, ffn_down, x1)


def _mixers(z3, p):
    B, S, _ = z3.shape
    qt, kt, vt = _foxprep(z3, p["fox_fb"], p["fox_qg"], p["fox_kg"])
    o_a = _fox_attention(qt, kt, vt)
    o_b = _convmix(z3, p["conv_dw"], p["conv_db"], p["conv_ln_g"], p["conv_ln_b"])
    o_c = _glamix(z3, p["gla_wa"], p["gla_ba"], p["gla_og"])
    o_d = _poolmix(z3, p["pool_w"], p["pool_scale"])
    return o_a, o_b, o_c, o_d


STACKED = ("w_cat", "w_branch", "w_out", "ffn_up", "ffn_down")


def _layer(x, params, l):
    B, S, D = x.shape
    T = B * S
    x2 = x.reshape(T, D)
    p = {k: (v if k in STACKED else v[l]) for k, v in params.items()}
    h, z = _inproj(x2, p["norm1_g"], p["w_cat"], l)
    outs = _mixers(z.reshape(B, S, Z_USED), p)
    y = _merge(h, [o.reshape(T, BRANCH_W) for o in outs], p["w_cat"], p["gate_b"], p["w_branch"], l)
    x1, h2 = _outproj(y, x2, p["w_out"], p["norm2_g"], l)
    g = _ffnup(h2, p["ffn_up"], p["ffn_dw"], p["ffn_db"], S, l)
    return _ffndown(g, p["ffn_down"], x1, l).reshape(B, S, D)


def _w_in_pieces(d_model):
    n_small = 3 * BRANCH_W + FOX_HEADS + 2 * BRANCH_W + 2 * GLA_DK + GLA_DV + GLA_RANK + GLA_DV + BRANCH_W
    o_ff = 3 * BRANCH_W
    o_cz = o_ff + FOX_HEADS
    o_ga = o_cz + 2 * BRANCH_W + 2 * GLA_DK + GLA_DV
    o_gr = o_ga + GLA_RANK
    return n_small, [(0, o_ff, Z_FQ), (o_cz, o_ga, Z_CA), (o_gr, n_small, Z_GR),
                     (o_ff, o_cz, Z_SM + SM_FF), (o_ga, o_gr, Z_SM + SM_GA),
                     (n_small, n_small + 4 * d_model, Z_COLS)]


RELAYOUT_CHUNK = 512


def _relayout_body(w_ref, o_ref, *, pieces):
    td = o_ref.shape[0]
    narrow = []
    for src0, src1, dst in pieces:
        if (src1 - src0) % RELAYOUT_CHUNK:
            narrow.append((src0, src1, dst))
            continue
        for c in range(0, src1 - src0, RELAYOUT_CHUNK):
            o_ref[:, dst + c:dst + c + RELAYOUT_CHUNK] = (
                w_ref[src0 + c:src0 + c + RELAYOUT_CHUNK, :].T.astype(o_ref.dtype))
    assert narrow[0][2] == Z_SM and all(a[2] + a[1] - a[0] == b[2] for a, b in zip(narrow, narrow[1:]))
    used = sum(s1 - s0 for s0, s1, _ in narrow)
    rows = [w_ref[s0:s1, :] for s0, s1, _ in narrow] + [jnp.zeros((LANES - used, td), f32)]
    o_ref[:, Z_SM:Z_SM + LANES] = jnp.concatenate(rows, axis=0).T.astype(o_ref.dtype)
    o_ref[:, Z_SM + LANES:Z_COLS] = jnp.zeros((td, Z_COLS - Z_SM - LANES), o_ref.dtype)


def _relayout_w_in(w_in):
    L, D, n_in = w_in.shape
    _, pieces = _w_in_pieces(D)
    n_out = Z_COLS + 4 * D
    td = 256
    return pl.pallas_call(
        functools.partial(_relayout_body, pieces=pieces),
        grid=(L, D // td),
        in_specs=[pl.BlockSpec((None, n_in, td), lambda l, i: (l, 0, i))],
        out_specs=pl.BlockSpec((None, td, n_out), lambda l, i: (l, i, 0)),
        out_shape=jax.ShapeDtypeStruct((L, D, n_out), MXU_DT),
        compiler_params=_cparams(("parallel", "parallel")),
        name="relayout_w_in",
    )(jnp.swapaxes(w_in, 1, 2))


def _prepare(norm1_g, w_in, fox_fb, fox_qg, fox_kg, conv_dw, conv_db, conv_ln_g, conv_ln_b,
             gla_wa, gla_ba, gla_og, pool_w, pool_scale, gate_b, w_branch, w_out,
             norm2_g, ffn_up, ffn_dw, ffn_db, ffn_down):
    w_cat = _relayout_w_in(w_in)
    row = lambda a: a[:, None, :]
    pad_lanes = lambda a, off: jnp.pad(a, ((0, 0), (off, LANES - off - a.shape[1])))
    wa_pad = jnp.pad(gla_wa, ((0, 0), (SM_GA, LANES - SM_GA - GLA_RANK), (0, 0)))
    return {
        "norm1_g": row(norm1_g),
        "w_cat": w_cat,
        "fox_fb": row(pad_lanes(fox_fb, SM_FF)),
        "fox_qg": row(jnp.tile(fox_qg, (1, FOX_HEADS))),
        "fox_kg": row(jnp.tile(fox_kg, (1, FOX_HEADS))),
        "conv_dw": jnp.pad(conv_dw, ((0, 0), (0, 1), (0, 0))),
        "conv_db": row(conv_db), "conv_ln_g": row(conv_ln_g), "conv_ln_b": row(conv_ln_b),
        "gla_wa": wa_pad.astype(MXU_DT), "gla_ba": row(gla_ba), "gla_og": row(gla_og),
        "pool_w": pool_w.astype(MXU_DT), "pool_scale": row(pool_scale),
        "gate_b": row(gate_b),
        "w_branch": w_branch.astype(MXU_DT),
        "w_out": w_out.astype(MXU_DT),
        "norm2_g": row(norm2_g),
        "ffn_up": ffn_up.astype(MXU_DT), "ffn_dw": ffn_dw, "ffn_db": row(ffn_db),
        "ffn_down": ffn_down.astype(MXU_DT),
    }


def kernel(x, norm1_g, w_in, fox_fb, fox_qg, fox_kg, conv_dw, conv_db, conv_ln_g, conv_ln_b, gla_wa, gla_ba, gla_og, pool_w, pool_scale, gate_b, w_branch, w_out, norm2_g, ffn_up, ffn_dw, ffn_db, ffn_down):
    params = _prepare(norm1_g, w_in, fox_fb, fox_qg, fox_kg, conv_dw, conv_db, conv_ln_g, conv_ln_b,
                      gla_wa, gla_ba, gla_og, pool_w, pool_scale, gate_b, w_branch, w_out,
                      norm2_g, ffn_up, ffn_dw, ffn_db, ffn_down)
    for l in range(w_in.shape[0]):
        x = _layer(x, params, l)
    return x
```

```python
import functools

import jax
import jax.numpy as jnp
import numpy as np
from jax import lax
from jax.experimental import pallas as pl
from jax.experimental.pallas import tpu as pltpu

f32 = jnp.float32
MXU_DT = jnp.bfloat16
HI = lax.Precision.HIGHEST

EPS = 1e-6
BRANCH_W = 512
FOX_HEADS = 8
FOX_HD = 64
CONV_K = 31
GLA_HEADS = 4
GLA_DK = 256
GLA_DV = 512
GLA_RANK = 16
GLA_TEMP = 16.0
GLA_CHUNK = 16
POOL_WINDOWS = (2, 4, 8, 16)
FFN_K = 3
LANES = 128

Z_FQ, Z_FK, Z_FV, Z_CA, Z_CG = 0, 512, 1024, 1536, 2048
Z_GQ, Z_GK, Z_GV, Z_GR, Z_PZ, Z_SM, Z_COLS = 2560, 2816, 3072, 3584, 4096, 4608, 5120
Z_USED = Z_SM + 128
SM_FF, SM_GA = 0, 8

NEG_BIG = -1e30
LOG2E = 1.4426950408889634
VMEM_LIMIT = 56 * 1024 * 1024


def _cparams(sem):
    return pltpu.CompilerParams(dimension_semantics=sem, vmem_limit_bytes=VMEM_LIMIT)


def _group(idx, size):
    assert size & (size - 1) == 0
    return idx >> (size.bit_length() - 1)


def _log_sigmoid(x):
    return jnp.minimum(x, 0.0) - jnp.log1p(jnp.exp(-jnp.abs(x)))


def _inproj_body(x_ref, g_ref, w_ref, h_ref, z_ref):
    x = x_ref[...]
    ms = jnp.mean(x * x, axis=-1, keepdims=True)
    h = (x * lax.rsqrt(ms + EPS) * g_ref[...]).astype(h_ref.dtype)
    h_ref[...] = h
    z_ref[...] = jnp.dot(h, w_ref[...], preferred_element_type=f32)


def _inproj(x2, g, w_cat, l):
    T, D = x2.shape
    tm = min(512, T)
    return pl.pallas_call(
        _inproj_body,
        grid=(T // tm,),
        in_specs=[pl.BlockSpec((tm, D), lambda i: (i, 0)),
                  pl.BlockSpec((1, D), lambda i: (0, 0)),
                  pl.BlockSpec((None, D, Z_USED), lambda i: (l, 0, 0), pipeline_mode=pl.Buffered(1))],
        out_specs=[pl.BlockSpec((tm, D), lambda i: (i, 0)),
                   pl.BlockSpec((tm, Z_USED), lambda i: (i, 0))],
        out_shape=[jax.ShapeDtypeStruct((T, D), MXU_DT),
                   jax.ShapeDtypeStruct((T, Z_USED), f32)],
        compiler_params=_cparams(("parallel",)),
        name="inproj",
    )(x2, g, w_cat)


def _head_rmsnorm(x, g, lo):
    sq = x * x
    s0 = jnp.sum(jnp.where(lo, sq, 0.0), axis=-1, keepdims=True)
    s1 = jnp.sum(jnp.where(lo, 0.0, sq), axis=-1, keepdims=True)
    r = jnp.where(lo, lax.rsqrt(s0 * (1.0 / FOX_HD) + EPS), lax.rsqrt(s1 * (1.0 / FOX_HD) + EPS))
    return x * r * g


N_SPLIT = 3
L_KC = FOX_HD
L_QC = FOX_HD + N_SPLIT
L_ONE = FOX_HD
V_ROWS = FOX_HD + 16


def _split_pieces(x):
    pieces, rest = [], x
    for _ in range(N_SPLIT):
        piece = rest.astype(jnp.bfloat16)
        pieces.append(piece)
        rest = rest - piece.astype(f32)
    return pieces


def _foxprep_body(q_ref, k_ref, v_ref, sm_ref, fb_ref, qg_ref, kg_ref,
                  qt_ref, kt_ref, vt_ref, carry_ref, *, tp):
    i = pl.program_id(1)

    @pl.when(i == 0)
    def _():
        carry_ref[...] = jnp.zeros_like(carry_ref)

    logf = _log_sigmoid(sm_ref[...] + fb_ref[...])
    row = lax.broadcasted_iota(jnp.int32, (tp, tp), 0)
    col = lax.broadcasted_iota(jnp.int32, (tp, tp), 1)
    tri = (col <= row).astype(f32)
    c = jnp.dot(tri, logf, preferred_element_type=f32, precision=HI) + carry_ref[0:1, :]
    carry_ref[0:1, :] = c[tp - 1:tp, :]

    lane = lax.broadcasted_iota(jnp.int32, (1, LANES), 1)
    lo = lane < FOX_HD
    ones_q = ((lane >= L_KC) & (lane < L_KC + N_SPLIT)).astype(f32)
    ones_k = ((lane >= L_QC) & (lane < L_QC + N_SPLIT)).astype(f32)
    ones_v = (lane == L_ONE).astype(f32)
    in_qc = (lane >= L_QC) & (lane < L_QC + N_SPLIT)
    in_kc = (lane >= L_KC) & (lane < L_KC + N_SPLIT)
    pieces = [p.astype(f32) for p in _split_pieces(c * LOG2E)]
    scale = FOX_HD ** -0.5 * LOG2E
    for pair in range(BRANCH_W // LANES):
        sl = slice(pair * LANES, (pair + 1) * LANES)
        qn = _head_rmsnorm(q_ref[:, sl], qg_ref[:, sl], lo) * scale
        kn = _head_rmsnorm(k_ref[:, sl], kg_ref[:, sl], lo)
        vv = v_ref[:, sl]
        for e in (0, 1):
            h = 2 * pair + e
            if e == 1:
                qn, kn, vv = (pltpu.roll(t, FOX_HD, axis=1) for t in (qn, kn, vv))
            cp = jnp.zeros((tp, LANES), f32)
            for j, piece in enumerate(pieces):
                pj = jnp.broadcast_to(piece[:, h:h + 1], (tp, LANES))
                cp = jnp.where((lane == L_KC + j) | (lane == L_QC + j), pj, cp)
            qt_ref[h] = jnp.where(lo, qn, jnp.where(in_qc, cp, ones_q)).astype(qt_ref.dtype)
            kt_ref[h] = jnp.where(lo, kn, jnp.where(in_kc, -cp, ones_k)).astype(kt_ref.dtype)
            vt_ref[h] = jnp.where(lo, vv, ones_v).T[0:V_ROWS, :].astype(vt_ref.dtype)


def _foxprep(z3, fb_pad, qg, kg):
    B, S, _ = z3.shape
    tp = min(512, S)
    blk = lambda w, c: pl.BlockSpec((None, tp, w), lambda b, i, c=c: (b, i, c))
    vec = lambda w: pl.BlockSpec((1, w), lambda b, i: (0, 0))
    head_blk = pl.BlockSpec((None, FOX_HEADS, tp, LANES), lambda b, i: (b, 0, i, 0))
    return pl.pallas_call(
        functools.partial(_foxprep_body, tp=tp),
        grid=(B, S // tp),
        in_specs=[blk(BRANCH_W, Z_FQ // BRANCH_W), blk(BRANCH_W, Z_FK // BRANCH_W),
                  blk(BRANCH_W, Z_FV // BRANCH_W), blk(LANES, Z_SM // LANES),
                  vec(LANES), vec(BRANCH_W), vec(BRANCH_W)],
        out_specs=[head_blk, head_blk,
                   pl.BlockSpec((None, FOX_HEADS, V_ROWS, tp), lambda b, i: (b, 0, 0, i))],
        out_shape=[jax.ShapeDtypeStruct((B, FOX_HEADS, S, LANES), MXU_DT)] * 2
        + [jax.ShapeDtypeStruct((B, FOX_HEADS, V_ROWS, S), MXU_DT)],
        scratch_shapes=[pltpu.VMEM((8, LANES), f32)],
        compiler_params=_cparams(("parallel", "arbitrary")),
        name="foxprep",
    )(z3, z3, z3, z3, fb_pad, qg, kg)


def _fox_body(q_ref, k_ref, v_ref, o_ref, m_ref, acc_ref, *, tq):
    i = pl.program_id(2)
    m_ref[...] = jnp.full_like(m_ref, NEG_BIG)
    acc_ref[...] = jnp.zeros_like(acc_ref)

    heads = range(FOX_GROUP)

    def blocks(j, masked):
        start = pl.multiple_of(j * tq, tq)
        s = [lax.dot_general(k_ref[g, pl.ds(start, tq), :], q_ref[g], (((1,), (1,)), ((), ())),
                             preferred_element_type=f32) for g in heads]
        if masked:
            key = lax.broadcasted_iota(jnp.int32, (tq, tq), 0)
            qry = lax.broadcasted_iota(jnp.int32, (tq, tq), 1)
            s = [jnp.where(key <= qry, sg, NEG_BIG) for sg in s]
        m_prev = [m_ref[g, 0:1, :] for g in heads]
        m_new = [jnp.maximum(m_prev[g], jnp.max(s[g], axis=0, keepdims=True)) for g in heads]
        p = [jnp.exp2(s[g] - m_new[g]).astype(q_ref.dtype) for g in heads]
        pv = [jnp.dot(v_ref[g, :, pl.ds(start, tq)], p[g], preferred_element_type=f32) for g in heads]
        for g in heads:
            m_ref[g, 0:1, :] = m_new[g]
            acc_ref[g] = jnp.exp2(m_prev[g] - m_new[g]) * acc_ref[g] + pv[g]

    def full_blocks(jj, carry):
        for u in range(FOX_UNROLL):
            blocks(FOX_UNROLL * jj + u, False)
        return carry

    lax.fori_loop(0, i // FOX_UNROLL, full_blocks, 0)
    for u in range(FOX_UNROLL - 1):
        @pl.when(i % FOX_UNROLL > u)
        def _():
            blocks(i - i % FOX_UNROLL + u, False)

    blocks(i, True)
    for pair in range(FOX_GROUP // 2):
        o_t = jnp.concatenate(
            [acc_ref[g, 0:FOX_HD, :] / acc_ref[g, L_ONE:L_ONE + 1, :] for g in (2 * pair, 2 * pair + 1)], axis=0)
        o_ref[:, pair * LANES:(pair + 1) * LANES] = o_t.T.astype(o_ref.dtype)


FOX_GROUP = 8
FOX_UNROLL = 4


def _fox_attention(qt, kt, vtt):
    B, H, S, _ = qt.shape
    tq = min(256, S)
    G = FOX_GROUP
    assert G == H
    resident = dict(pipeline_mode=pl.Buffered(1))
    return pl.pallas_call(
        functools.partial(_fox_body, tq=tq),
        grid=(B, H // G, S // tq),
        in_specs=[pl.BlockSpec((None, G, tq, LANES), lambda b, h, i: (b, h, i, 0)),
                  pl.BlockSpec((None, G, S, LANES), lambda b, h, i: (b, h, 0, 0), **resident),
                  pl.BlockSpec((None, G, V_ROWS, S), lambda b, h, i: (b, h, 0, 0), **resident)],
        out_specs=pl.BlockSpec((None, tq, H * FOX_HD), lambda b, h, i: (b, i, 0)),
        out_shape=jax.ShapeDtypeStruct((B, S, H * FOX_HD), MXU_DT),
        scratch_shapes=[pltpu.VMEM((G, 8, tq), f32), pltpu.VMEM((G, V_ROWS, tq), f32)],
        compiler_params=_cparams(("parallel", "parallel", "arbitrary")),
        name="fox_attn",
    )(qt, kt, vtt)


CONV_HALO = 32
CONV_ROWS = 64


def _convmix_body(a_ref, g_ref, dw_ref, db_ref, lng_ref, lnb_ref, o_ref, u_ref, us_ref, *, tp):
    @pl.when(pl.program_id(1) == 0)
    def _():
        u_ref[0:CONV_HALO, :] = jnp.zeros((CONV_HALO, BRANCH_W), f32)

    u_ref[CONV_HALO:CONV_HALO + tp, :] = a_ref[...] * jax.nn.sigmoid(g_ref[...])
    n_shift = CONV_HALO + tp - 8
    for rho in range(1, 8):
        us_ref[rho - 1, 0:n_shift, :] = u_ref[rho:rho + n_shift, :]
    for r in range(0, tp, CONV_ROWS):
        acc = jnp.broadcast_to(db_ref[...], (CONV_ROWS, BRANCH_W))
        for k in range(CONV_K):
            q8, rho = divmod(CONV_HALO - (CONV_K - 1) + k, 8)
            off = r + 8 * q8
            tap = u_ref[off:off + CONV_ROWS, :] if rho == 0 else us_ref[rho - 1, off:off + CONV_ROWS, :]
            acc = acc + dw_ref[k:k + 1, :] * tap
        mu = jnp.mean(acc, axis=-1, keepdims=True)
        d = acc - mu
        var = jnp.mean(d * d, axis=-1, keepdims=True)
        y = d * lax.rsqrt(var + EPS) * lng_ref[...] + lnb_ref[...]
        o_ref[r:r + CONV_ROWS, :] = (y * jax.nn.sigmoid(y)).astype(o_ref.dtype)
    u_ref[0:CONV_HALO, :] = u_ref[tp:tp + CONV_HALO, :]


def _convmix(z3, dw_pad, db, lng, lnb):
    B, S, _ = z3.shape
    tp = min(256, S)
    blk = lambda c: pl.BlockSpec((None, tp, BRANCH_W), lambda b, i, c=c: (b, i, c))
    vec = pl.BlockSpec((1, BRANCH_W), lambda b, i: (0, 0))
    return pl.pallas_call(
        functools.partial(_convmix_body, tp=tp),
        grid=(B, S // tp),
        in_specs=[blk(Z_CA // BRANCH_W), blk(Z_CG // BRANCH_W),
                  pl.BlockSpec(dw_pad.shape, lambda b, i: (0, 0)), vec, vec, vec],
        out_specs=blk(0),
        out_shape=jax.ShapeDtypeStruct((B, S, BRANCH_W), MXU_DT),
        scratch_shapes=[pltpu.VMEM((CONV_HALO + tp, BRANCH_W), f32),
                        pltpu.VMEM((7, CONV_HALO + tp, BRANCH_W), f32)],
        compiler_params=_cparams(("parallel", "arbitrary")),
        name="convmix",
    )(z3, z3, dw_pad, db, lng, lnb)


POOL_HALO = 16


def _poolmix_body(u_in_ref, pw_ref, sc_ref, o_ref, u_ref, *, tp):
    i = pl.program_id(1)

    @pl.when(i == 0)
    def _():
        u_ref[0:POOL_HALO, :] = jnp.zeros((POOL_HALO, BRANCH_W), f32)

    u_ref[POOL_HALO:POOL_HALO + tp, :] = u_in_ref[...]
    pos = i * tp + lax.broadcasted_iota(jnp.int32, (tp, 1), 0)
    for gi, w in enumerate(POOL_WINDOWS):
        sl = slice(gi * LANES, (gi + 1) * LANES)
        acc = u_ref[POOL_HALO:POOL_HALO + tp, sl]
        for j in range(1, w):
            acc = acc + u_ref[POOL_HALO - j:POOL_HALO - j + tp, sl]
        cnt = jnp.minimum(pos + 1, w).astype(f32)
        mixed = acc / cnt - u_ref[POOL_HALO:POOL_HALO + tp, sl]
        out = jnp.dot(mixed.astype(pw_ref.dtype), pw_ref[gi], preferred_element_type=f32)
        o_ref[:, sl] = (out * sc_ref[:, sl]).astype(o_ref.dtype)
    u_ref[0:POOL_HALO, :] = u_ref[tp:tp + POOL_HALO, :]


def _poolmix(z3, pw, scale):
    B, S, _ = z3.shape
    tp = min(512, S)
    return pl.pallas_call(
        functools.partial(_poolmix_body, tp=tp),
        grid=(B, S // tp),
        in_specs=[pl.BlockSpec((None, tp, BRANCH_W), lambda b, i: (b, i, Z_PZ // BRANCH_W)),
                  pl.BlockSpec(pw.shape, lambda b, i: (0, 0, 0)),
                  pl.BlockSpec((1, BRANCH_W), lambda b, i: (0, 0))],
        out_specs=pl.BlockSpec((None, tp, BRANCH_W), lambda b, i: (b, i, 0)),
        out_shape=jax.ShapeDtypeStruct((B, S, BRANCH_W), MXU_DT),
        scratch_shapes=[pltpu.VMEM((POOL_HALO + tp, BRANCH_W), f32)],
        compiler_params=_cparams(("parallel", "arbitrary")),
        name="poolmix",
    )(z3, pw, scale)


def _gla_body(q_ref, k_ref, v_ref, r_ref, sm_ref, wa_ref, ba_ref, og_ref, sel_ref, o_ref,
              st_ref, kb_ref, bb_ref, qlo_ref, qhi_ref, kd_ref, vh_ref, dec_ref, oi_ref, up_ref, sb_ref,
              *, tg):
    C = GLA_CHUNK
    hk = GLA_DK // GLA_HEADS
    hv = GLA_DV // GLA_HEADS

    @pl.when(pl.program_id(1) == 0)
    def _():
        st_ref[...] = jnp.zeros_like(st_ref)

    pre = jnp.dot(sm_ref[...].astype(wa_ref.dtype), wa_ref[...], preferred_element_type=f32) + ba_ref[...]
    loga = _log_sigmoid(pre) * (LOG2E / GLA_TEMP)
    row = lax.broadcasted_iota(jnp.int32, (tg, tg), 0)
    col = lax.broadcasted_iota(jnp.int32, (tg, tg), 1)
    same = _group(row, C) == _group(col, C)
    bc = jnp.dot((same & (col <= row)).astype(f32), loga, preferred_element_type=f32, precision=HI)
    blast = jnp.dot(same.astype(f32), loga, preferred_element_type=f32, precision=HI)

    q = q_ref[...] * (hk ** -0.5)
    k = k_ref[...]
    qd = q * jnp.exp2(bc)
    first = (lax.broadcasted_iota(jnp.int32, (1, GLA_DK), 1) & (LANES - 1)) < hk
    qlo_ref[...] = jnp.where(first, qd, 0.0).astype(qlo_ref.dtype)
    qhi_ref[...] = jnp.where(first, 0.0, qd).astype(qhi_ref.dtype)
    kd_ref[...] = (k * jnp.exp2(blast - bc)).astype(kd_ref.dtype)
    dec_ref[...] = jnp.exp2(blast)
    vh_ref[...] = v_ref[...].astype(vh_ref.dtype)

    kb_ref[0:C, :] = jnp.zeros((C, GLA_DK), f32)
    bb_ref[0:C, :] = jnp.zeros((C, GLA_DK), f32)
    kb_ref[C:C + tg, :] = k
    bb_ref[C:C + tg, :] = bc
    rpos = lax.broadcasted_iota(jnp.int32, (tg, 1), 0) & (C - 1)
    lo = lax.broadcasted_iota(jnp.int32, (1, LANES), 1) < hk
    st = [st_ref[p] for p in range(GLA_HEADS // 2)]

    def intra(delta, acc):
        valid = rpos >= delta
        ks = kb_ref[C - delta:C - delta + tg, :]
        bs = bb_ref[C - delta:C - delta + tg, :]
        w = jnp.where(valid, q * ks * jnp.exp2(jnp.where(valid, bc - bs, 0.0)), 0.0)
        return acc + jnp.dot(w.astype(MXU_DT), sel_ref[delta], preferred_element_type=f32)

    n_steps = tg // C
    pairs = range(GLA_HEADS // 2)

    def increment(n):
        rows = slice(n * C, (n + 1) * C)
        for p in pairs:
            ps = slice(p * LANES, (p + 1) * LANES)
            upd = [lax.dot_general(vh_ref[rows, (2 * p + e) * hv:(2 * p + e + 1) * hv], kd_ref[rows, ps],
                                   (((0,), (0,)), ((), ())), preferred_element_type=f32) for e in (0, 1)]
            up_ref[n, p] = jnp.where(lo, upd[0], upd[1])

    def read_out(n):
        rows = slice(n * C, (n + 1) * C)
        for p in pairs:
            ps = slice(p * LANES, (p + 1) * LANES)
            lhs = jnp.concatenate([qlo_ref[rows, ps], qhi_ref[rows, ps]], axis=0)
            o_p = lax.dot_general(lhs, sb_ref[n, p], (((1,), (1,)), ((), ())),
                                  preferred_element_type=f32)
            for e in (0, 1):
                oi_ref[rows, (2 * p + e) * hv:(2 * p + e + 1) * hv] = o_p[e * C:(e + 1) * C]

    per_it = 2 * n_steps // C
    attn = jnp.zeros((tg, GLA_DK), f32)
    for it in range(C // 2):
        attn = intra(it, attn)
        for n in range(it * per_it, min((it + 1) * per_it, n_steps)):
            increment(n)
    for n in range(n_steps):
        for p in pairs:
            sb_ref[n, p] = st[p].astype(sb_ref.dtype)
            st[p] = st[p] * dec_ref[n * C:n * C + 1, p * LANES:(p + 1) * LANES] + up_ref[n, p]
    for p in pairs:
        st_ref[p] = st[p]
    for it in range(C // 2, C):
        attn = intra(it, attn)
        for n in range((it - C // 2) * per_it, min((it - C // 2 + 1) * per_it, n_steps)):
            read_out(n)

    assert tg == GLA_DK
    lane_t = lax.broadcasted_iota(jnp.int32, (1, tg), 1)
    for h in range(GLA_HEADS):
        mine = attn if h == 0 else pltpu.roll(attn, tg - h * hk, axis=1)
        band = pltpu.roll(jnp.where(lane_t < C, mine, 0.0), tg - (C - 1), axis=1, stride=1, stride_axis=0)
        vs = slice(h * hv, (h + 1) * hv)
        oi_ref[:, vs] += jnp.dot(band.astype(MXU_DT), vh_ref[:, vs], preferred_element_type=f32)

    for h in range(GLA_HEADS):
        sl = slice(h * hv, (h + 1) * hv)
        oh = oi_ref[:, sl]
        oh = oh * lax.rsqrt(jnp.mean(oh * oh, axis=-1, keepdims=True) + EPS) * og_ref[:, sl]
        rr = r_ref[:, sl]
        o_ref[:, sl] = (oh * (rr * jax.nn.sigmoid(rr))).astype(o_ref.dtype)


def _gla_selectors():
    hk = GLA_DK // GLA_HEADS
    lane = np.arange(GLA_DK)
    sel = np.zeros((GLA_CHUNK, GLA_DK, GLA_DK), np.float32)
    for delta in range(GLA_CHUNK):
        sel[delta, lane, (lane // hk) * hk + (GLA_CHUNK - 1 - delta)] = 1.0
    return jnp.asarray(sel, MXU_DT)


def _glamix(z3, wa_pad, ba, og):
    B, S, _ = z3.shape
    tg = min(256, S)
    blk = lambda w, c: pl.BlockSpec((None, tg, w), lambda b, i, c=c: (b, i, c))
    vec = lambda w: pl.BlockSpec((1, w), lambda b, i: (0, 0))
    return pl.pallas_call(
        functools.partial(_gla_body, tg=tg),
        grid=(B, S // tg),
        in_specs=[blk(GLA_DK, Z_GQ // GLA_DK), blk(GLA_DK, Z_GK // GLA_DK),
                  blk(GLA_DV, Z_GV // GLA_DV), blk(GLA_DV, Z_GR // GLA_DV),
                  blk(LANES, Z_SM // LANES),
                  pl.BlockSpec((LANES, GLA_DK), lambda b, i: (0, 0)), vec(GLA_DK), vec(GLA_DV),
                  pl.BlockSpec((GLA_CHUNK, GLA_DK, GLA_DK), lambda b, i: (0, 0, 0))],
        out_specs=blk(GLA_DV, 0),
        out_shape=jax.ShapeDtypeStruct((B, S, GLA_DV), MXU_DT),
        scratch_shapes=[pltpu.VMEM((GLA_HEADS // 2, GLA_DV // GLA_HEADS, LANES), f32),
                        pltpu.VMEM((GLA_CHUNK + tg, GLA_DK), f32),
                        pltpu.VMEM((GLA_CHUNK + tg, GLA_DK), f32),
                        pltpu.VMEM((tg, GLA_DK), MXU_DT),
                        pltpu.VMEM((tg, GLA_DK), MXU_DT),
                        pltpu.VMEM((tg, GLA_DK), MXU_DT),
                        pltpu.VMEM((tg, GLA_DV), MXU_DT),
                        pltpu.VMEM((tg, GLA_DK), f32),
                        pltpu.VMEM((tg, GLA_DV), f32),
                        pltpu.VMEM((tg // GLA_CHUNK, GLA_HEADS // 2, GLA_DV // GLA_HEADS, LANES), f32),
                        pltpu.VMEM((tg // GLA_CHUNK, GLA_HEADS // 2, GLA_DV // GLA_HEADS, LANES), MXU_DT)],
        compiler_params=_cparams(("parallel", "arbitrary")),
        name="glamix",
    )(z3, z3, z3, z3, z3, wa_pad, ba, og, _gla_selectors())


def _merge_body(h_ref, oa_ref, ob_ref, oc_ref, od_ref, wg0, wg1, wg2, wg3, gb0, gb1, gb2, gb3,
                p0, p1, p2, p3, y_ref):
    h = h_ref[...]
    acc = None
    for o_ref, wg, gb, p in ((oa_ref, wg0, gb0, p0), (ob_ref, wg1, gb1, p1),
                             (oc_ref, wg2, gb2, p2), (od_ref, wg3, gb3, p3)):
        gate = jax.nn.sigmoid(jnp.dot(h, wg[...], preferred_element_type=f32) + gb[...])
        term = gate * jnp.dot(o_ref[...], p[...], preferred_element_type=f32)
        acc = term if acc is None else acc + term
    y_ref[...] = acc.astype(y_ref.dtype)


def _merge(h, outs, w_gate, gate_b, w_branch, l):
    T, D = h.shape
    tm, tn = min(1024, T), 512
    nj = D // tn
    g0 = Z_COLS // tn
    tok = lambda w: pl.BlockSpec((tm, w), lambda i, j: (i, 0))
    wg = [pl.BlockSpec((None, D, tn), lambda i, j, b=b: (l, 0, g0 + b * nj + j)) for b in range(4)]
    gb = [pl.BlockSpec((1, tn), lambda i, j, b=b: (0, b * nj + j)) for b in range(4)]
    pb = [pl.BlockSpec((None, None, BRANCH_W, tn), lambda i, j, b=b: (l, b, 0, j)) for b in range(4)]
    return pl.pallas_call(
        _merge_body,
        grid=(T // tm, nj),
        in_specs=[tok(D)] + [tok(BRANCH_W)] * 4 + wg + gb + pb,
        out_specs=pl.BlockSpec((tm, tn), lambda i, j: (i, j)),
        out_shape=jax.ShapeDtypeStruct((T, D), MXU_DT),
        compiler_params=_cparams(("parallel", "arbitrary")),
        name="merge",
    )(h, *outs, *([w_gate] * 4), *([gate_b] * 4), *([w_branch] * 4))


def _outproj_body(y_ref, x_ref, w_ref, g_ref, x1_ref, h2_ref):
    x1 = x_ref[...] + jnp.dot(y_ref[...], w_ref[...], preferred_element_type=f32)
    x1_ref[...] = x1
    ms = jnp.mean(x1 * x1, axis=-1, keepdims=True)
    h2_ref[...] = (x1 * lax.rsqrt(ms + EPS) * g_ref[...]).astype(h2_ref.dtype)


def _outproj(y, x2, w_out, g2, l):
    T, D = x2.shape
    tm = min(512, T)
    tok = pl.BlockSpec((tm, D), lambda i: (i, 0))
    return pl.pallas_call(
        _outproj_body,
        grid=(T // tm,),
        in_specs=[tok, tok, pl.BlockSpec((None, D, D), lambda i: (l, 0, 0)),
                  pl.BlockSpec((1, D), lambda i: (0, 0))],
        out_specs=[tok, tok],
        out_shape=[jax.ShapeDtypeStruct((T, D), f32), jax.ShapeDtypeStruct((T, D), MXU_DT)],
        compiler_params=_cparams(("parallel",)),
        name="outproj",
    )(y, x2, w_out, g2)


FFN_HALO = 8


def _ffnup_body(h_ref, wa_ref, wv_ref, dwa_ref, dwv_ref, dba_ref, dbv_ref, g_ref, ua_ref, uv_ref,
                *, tm, tiles_per_seq):
    @pl.when(pl.program_id(1) % tiles_per_seq == 0)
    def _():
        ua_ref[0:FFN_HALO, :] = jnp.zeros((FFN_HALO, ua_ref.shape[1]), f32)
        uv_ref[0:FFN_HALO, :] = jnp.zeros((FFN_HALO, uv_ref.shape[1]), f32)

    h = h_ref[...]

    def conv(w_ref, dw_ref, db_ref, u_ref):
        u_ref[FFN_HALO:FFN_HALO + tm, :] = jnp.dot(h, w_ref[...], preferred_element_type=f32)
        y = db_ref[...] + dw_ref[FFN_K - 1:FFN_K, :] * u_ref[FFN_HALO:FFN_HALO + tm, :]
        for k in range(FFN_K - 1):
            off = FFN_HALO - (FFN_K - 1) + k
            y = y + dw_ref[k:k + 1, :] * u_ref[off:off + tm, :]
        u_ref[0:FFN_HALO, :] = u_ref[tm:tm + FFN_HALO, :]
        return y

    a = conv(wa_ref, dwa_ref, dba_ref, ua_ref)
    v = conv(wv_ref, dwv_ref, dbv_ref, uv_ref)
    g_ref[...] = (a * jax.nn.sigmoid(a) * v).astype(g_ref.dtype)


def _ffnup(h2, ffn_up, ffn_dw, ffn_db, seq_len, l):
    T, D = h2.shape
    dff = ffn_up.shape[2] // 2
    tm, tn = min(512, seq_len), dff // 2
    nj = dff // tn
    resident = dict(pipeline_mode=pl.Buffered(1))
    return pl.pallas_call(
        functools.partial(_ffnup_body, tm=tm, tiles_per_seq=seq_len // tm),
        grid=(nj, T // tm),
        in_specs=[pl.BlockSpec((tm, D), lambda j, i: (i, 0)),
                  pl.BlockSpec((None, D, tn), lambda j, i: (l, 0, j), **resident),
                  pl.BlockSpec((None, D, tn), lambda j, i: (l, 0, nj + j), **resident),
                  pl.BlockSpec((FFN_K, tn), lambda j, i: (0, j)),
                  pl.BlockSpec((FFN_K, tn), lambda j, i: (0, nj + j)),
                  pl.BlockSpec((1, tn), lambda j, i: (0, j)),
                  pl.BlockSpec((1, tn), lambda j, i: (0, nj + j))],
        out_specs=pl.BlockSpec((tm, tn), lambda j, i: (i, j)),
        out_shape=jax.ShapeDtypeStruct((T, dff), MXU_DT),
        scratch_shapes=[pltpu.VMEM((FFN_HALO + tm, tn), f32), pltpu.VMEM((FFN_HALO + tm, tn), f32)],
        compiler_params=_cparams(("parallel", "arbitrary")),
        name="ffnup",
    )(h2, ffn_up, ffn_up, ffn_dw, ffn_dw, ffn_db, ffn_db)


def _ffndown_body(g_ref, w_ref, x_ref, o_ref):
    o_ref[...] = x_ref[...] + jnp.dot(g_ref[...], w_ref[...], preferred_element_type=f32)


def _ffndown(g, ffn_down, x1, l):
    T, D = x1.shape
    dff = g.shape[1]
    tm = min(512, T)
    return pl.pallas_call(
        _ffndown_body,
        grid=(T // tm,),
        in_specs=[pl.BlockSpec((tm, dff), lambda i: (i, 0)),
                  pl.BlockSpec((None, dff, D), lambda i: (l, 0, 0), pipeline_mode=pl.Buffered(1)),
                  pl.BlockSpec((tm, D), lambda i: (i, 0))],
        out_specs=pl.BlockSpec((tm, D), lambda i: (i, 0)),
        out_shape=jax.ShapeDtypeStruct((T, D), f32),
        compiler_params=_cparams(("parallel",)),
        name="ffndown",
    )(g, ffn_down, x1)


def _mixers(z3, p):
    B, S, _ = z3.shape
    qt, kt, vt = _foxprep(z3, p["fox_fb"], p["fox_qg"], p["fox_kg"])
    o_a = _fox_attention(qt, kt, vt)
    o_b = _convmix(z3, p["conv_dw"], p["conv_db"], p["conv_ln_g"], p["conv_ln_b"])
    o_c = _glamix(z3, p["gla_wa"], p["gla_ba"], p["gla_og"])
    o_d = _poolmix(z3, p["pool_w"], p["pool_scale"])
    return o_a, o_b, o_c, o_d


STACKED = ("w_cat", "w_branch", "w_out", "ffn_up", "ffn_down")


def _layer(x, params, l):
    B, S, D = x.shape
    T = B * S
    x2 = x.reshape(T, D)
    p = {k: (v if k in STACKED else v[l]) for k, v in params.items()}
    h, z = _inproj(x2, p["norm1_g"], p["w_cat"], l)
    outs = _mixers(z.reshape(B, S, Z_USED), p)
    y = _merge(h, [o.reshape(T, BRANCH_W) for o in outs], p["w_cat"], p["gate_b"], p["w_branch"], l)
    x1, h2 = _outproj(y, x2, p["w_out"], p["norm2_g"], l)
    g = _ffnup(h2, p["ffn_up"], p["ffn_dw"], p["ffn_db"], S, l)
    return _ffndown(g, p["ffn_down"], x1, l).reshape(B, S, D)


def _w_in_pieces(d_model):
    n_small = 3 * BRANCH_W + FOX_HEADS + 2 * BRANCH_W + 2 * GLA_DK + GLA_DV + GLA_RANK + GLA_DV + BRANCH_W
    o_ff = 3 * BRANCH_W
    o_cz = o_ff + FOX_HEADS
    o_ga = o_cz + 2 * BRANCH_W + 2 * GLA_DK + GLA_DV
    o_gr = o_ga + GLA_RANK
    return n_small, [(0, o_ff, Z_FQ), (o_cz, o_ga, Z_CA), (o_gr, n_small, Z_GR),
                     (o_ff, o_cz, Z_SM + SM_FF), (o_ga, o_gr, Z_SM + SM_GA),
                     (n_small, n_small + 4 * d_model, Z_COLS)]


RELAYOUT_CHUNK = 512


def _relayout_body(w_ref, o_ref, *, pieces):
    td = o_ref.shape[0]
    narrow = []
    for src0, src1, dst in pieces:
        if (src1 - src0) % RELAYOUT_CHUNK:
            narrow.append((src0, src1, dst))
            continue
        for c in range(0, src1 - src0, RELAYOUT_CHUNK):
            o_ref[:, dst + c:dst + c + RELAYOUT_CHUNK] = (
                w_ref[src0 + c:src0 + c + RELAYOUT_CHUNK, :].T.astype(o_ref.dtype))
    assert narrow[0][2] == Z_SM and all(a[2] + a[1] - a[0] == b[2] for a, b in zip(narrow, narrow[1:]))
    used = sum(s1 - s0 for s0, s1, _ in narrow)
    rows = [w_ref[s0:s1, :] for s0, s1, _ in narrow] + [jnp.zeros((LANES - used, td), f32)]
    o_ref[:, Z_SM:Z_SM + LANES] = jnp.concatenate(rows, axis=0).T.astype(o_ref.dtype)
    o_ref[:, Z_SM + LANES:Z_COLS] = jnp.zeros((td, Z_COLS - Z_SM - LANES), o_ref.dtype)


def _relayout_w_in(w_in):
    L, D, n_in = w_in.shape
    _, pieces = _w_in_pieces(D)
    n_out = Z_COLS + 4 * D
    td = 256
    return pl.pallas_call(
        functools.partial(_relayout_body, pieces=pieces),
        grid=(L, D // td),
        in_specs=[pl.BlockSpec((None, n_in, td), lambda l, i: (l, 0, i))],
        out_specs=pl.BlockSpec((None, td, n_out), lambda l, i: (l, i, 0)),
        out_shape=jax.ShapeDtypeStruct((L, D, n_out), MXU_DT),
        compiler_params=_cparams(("parallel", "parallel")),
        name="relayout_w_in",
    )(jnp.swapaxes(w_in, 1, 2))


def _prepare(norm1_g, w_in, fox_fb, fox_qg, fox_kg, conv_dw, conv_db, conv_ln_g, conv_ln_b,
             gla_wa, gla_ba, gla_og, pool_w, pool_scale, gate_b, w_branch, w_out,
             norm2_g, ffn_up, ffn_dw, ffn_db, ffn_down):
    w_cat = _relayout_w_in(w_in)
    row = lambda a: a[:, None, :]
    pad_lanes = lambda a, off: jnp.pad(a, ((0, 0), (off, LANES - off - a.shape[1])))
    wa_pad = jnp.pad(gla_wa, ((0, 0), (SM_GA, LANES - SM_GA - GLA_RANK), (0, 0)))
    return {
        "norm1_g": row(norm1_g),
        "w_cat": w_cat,
        "fox_fb": row(pad_lanes(fox_fb, SM_FF)),
        "fox_qg": row(jnp.tile(fox_qg, (1, FOX_HEADS))),
        "fox_kg": row(jnp.tile(fox_kg, (1, FOX_HEADS))),
        "conv_dw": jnp.pad(conv_dw, ((0, 0), (0, 1), (0, 0))),
        "conv_db": row(conv_db), "conv_ln_g": row(conv_ln_g), "conv_ln_b": row(conv_ln_b),
        "gla_wa": wa_pad.astype(MXU_DT), "gla_ba": row(gla_ba), "gla_og": row(gla_og),
        "pool_w": pool_w.astype(MXU_DT), "pool_scale": row(pool_scale),
        "gate_b": row(gate_b),
        "w_branch": w_branch.astype(MXU_DT),
        "w_out": w_out.astype(MXU_DT),
        "norm2_g": row(norm2_g),
        "ffn_up": ffn_up.astype(MXU_DT), "ffn_dw": ffn_dw, "ffn_db": row(ffn_db),
        "ffn_down": ffn_down.astype(MXU_DT),
    }


def kernel(x, norm1_g, w_in, fox_fb, fox_qg, fox_kg, conv_dw, conv_db, conv_ln_g, conv_ln_b, gla_wa, gla_ba, gla_og, pool_w, pool_scale, gate_b, w_branch, w_out, norm2_g, ffn_up, ffn_dw, ffn_db, ffn_down):
    params = _prepare(norm1_g, w_in, fox_fb, fox_qg, fox_kg, conv_dw, conv_db, conv_ln_g, conv_ln_b,
                      gla_wa, gla_ba, gla_og, pool_w, pool_scale, gate_b, w_branch, w_out,
                      norm2_g, ffn_up, ffn_dw, ffn_db, ffn_down)
    for l in range(w_in.shape[0]):
        x = _layer(x, params, l)
    return x
```

```python
import functools

import jax
import jax.numpy as jnp
import numpy as np
from jax import lax
from jax.experimental import pallas as pl
from jax.experimental.pallas import tpu as pltpu

f32 = jnp.float32
MXU_DT = jnp.bfloat16
HI = lax.Precision.HIGHEST

EPS = 1e-6
BRANCH_W = 512
FOX_HEADS = 8
FOX_HD = 64
CONV_K = 31
GLA_HEADS = 4
GLA_DK = 256
GLA_DV = 512
GLA_RANK = 16
GLA_TEMP = 16.0
GLA_CHUNK = 16
POOL_WINDOWS = (2, 4, 8, 16)
FFN_K = 3
LANES = 128

Z_FQ, Z_FK, Z_FV, Z_CA, Z_CG = 0, 512, 1024, 1536, 2048
Z_GQ, Z_GK, Z_GV, Z_GR, Z_PZ, Z_SM, Z_COLS = 2560, 2816, 3072, 3584, 4096, 4608, 5120
Z_USED = Z_SM + 128
SM_FF, SM_GA = 0, 8

NEG_BIG = -1e30
LOG2E = 1.4426950408889634
VMEM_LIMIT = 56 * 1024 * 1024


def _cparams(sem):
    return pltpu.CompilerParams(dimension_semantics=sem, vmem_limit_bytes=VMEM_LIMIT)


def _group(idx, size):
    assert size & (size - 1) == 0
    return idx >> (size.bit_length() - 1)


def _log_sigmoid(x):
    return jnp.minimum(x, 0.0) - jnp.log1p(jnp.exp(-jnp.abs(x)))


def _inproj_body(x_ref, g_ref, w_ref, h_ref, z_ref):
    x = x_ref[...]
    ms = jnp.mean(x * x, axis=-1, keepdims=True)
    h = (x * lax.rsqrt(ms + EPS) * g_ref[...]).astype(h_ref.dtype)
    h_ref[...] = h
    z_ref[...] = jnp.dot(h, w_ref[...], preferred_element_type=f32)


def _inproj(x2, g, w_cat, l):
    T, D = x2.shape
    tm = min(512, T)
    return pl.pallas_call(
        _inproj_body,
        grid=(T // tm,),
        in_specs=[pl.BlockSpec((tm, D), lambda i: (i, 0)),
                  pl.BlockSpec((1, D), lambda i: (0, 0)),
                  pl.BlockSpec((None, D, Z_USED), lambda i: (l, 0, 0), pipeline_mode=pl.Buffered(1))],
        out_specs=[pl.BlockSpec((tm, D), lambda i: (i, 0)),
                   pl.BlockSpec((tm, Z_USED), lambda i: (i, 0))],
        out_shape=[jax.ShapeDtypeStruct((T, D), MXU_DT),
                   jax.ShapeDtypeStruct((T, Z_USED), f32)],
        compiler_params=_cparams(("parallel",)),
        name="inproj",
    )(x2, g, w_cat)


def _head_rmsnorm(x, g, lo):
    sq = x * x
    s0 = jnp.sum(jnp.where(lo, sq, 0.0), axis=-1, keepdims=True)
    s1 = jnp.sum(jnp.where(lo, 0.0, sq), axis=-1, keepdims=True)
    r = jnp.where(lo, lax.rsqrt(s0 * (1.0 / FOX_HD) + EPS), lax.rsqrt(s1 * (1.0 / FOX_HD) + EPS))
    return x * r * g


N_SPLIT = 3
L_KC = FOX_HD
L_QC = FOX_HD + N_SPLIT
L_ONE = FOX_HD
V_ROWS = FOX_HD + 16


def _split_pieces(x):
    pieces, rest = [], x
    for _ in range(N_SPLIT):
        piece = rest.astype(jnp.bfloat16)
        pieces.append(piece)
        rest = rest - piece.astype(f32)
    return pieces


def _foxprep_body(q_ref, k_ref, v_ref, sm_ref, fb_ref, qg_ref, kg_ref,
                  qt_ref, kt_ref, vt_ref, carry_ref, *, tp):
    i = pl.program_id(1)

    @pl.when(i == 0)
    def _():
        carry_ref[...] = jnp.zeros_like(carry_ref)

    logf = _log_sigmoid(sm_ref[...] + fb_ref[...])
    row = lax.broadcasted_iota(jnp.int32, (tp, tp), 0)
    col = lax.broadcasted_iota(jnp.int32, (tp, tp), 1)
    tri = (col <= row).astype(f32)
    c = jnp.dot(tri, logf, preferred_element_type=f32, precision=HI) + carry_ref[0:1, :]
    carry_ref[0:1, :] = c[tp - 1:tp, :]

    lane = lax.broadcasted_iota(jnp.int32, (1, LANES), 1)
    lo = lane < FOX_HD
    ones_q = ((lane >= L_KC) & (lane < L_KC + N_SPLIT)).astype(f32)
    ones_k = ((lane >= L_QC) & (lane < L_QC + N_SPLIT)).astype(f32)
    ones_v = (lane == L_ONE).astype(f32)
    in_qc = (lane >= L_QC) & (lane < L_QC + N_SPLIT)
    in_kc = (lane >= L_KC) & (lane < L_KC + N_SPLIT)
    pieces = [p.astype(f32) for p in _split_pieces(c * LOG2E)]
    scale = FOX_HD ** -0.5 * LOG2E
    for pair in range(BRANCH_W // LANES):
        sl = slice(pair * LANES, (pair + 1) * LANES)
        qn = _head_rmsnorm(q_ref[:, sl], qg_ref[:, sl], lo) * scale
        kn = _head_rmsnorm(k_ref[:, sl], kg_ref[:, sl], lo)
        vv = v_ref[:, sl]
        for e in (0, 1):
            h = 2 * pair + e
            if e == 1:
                qn, kn, vv = (pltpu.roll(t, FOX_HD, axis=1) for t in (qn, kn, vv))
            cp = jnp.zeros((tp, LANES), f32)
            for j, piece in enumerate(pieces):
                pj = jnp.broadcast_to(piece[:, h:h + 1], (tp, LANES))
                cp = jnp.where((lane == L_KC + j) | (lane == L_QC + j), pj, cp)
            qt_ref[h] = jnp.where(lo, qn, jnp.where(in_qc, cp, ones_q)).astype(qt_ref.dtype)
            kt_ref[h] = jnp.where(lo, kn, jnp.where(in_kc, -cp, ones_k)).astype(kt_ref.dtype)
            vt_ref[h] = jnp.where(lo, vv, ones_v).T[0:V_ROWS, :].astype(vt_ref.dtype)


def _foxprep(z3, fb_pad, qg, kg):
    B, S, _ = z3.shape
    tp = min(512, S)
    blk = lambda w, c: pl.BlockSpec((None, tp, w), lambda b, i, c=c: (b, i, c))
    vec = lambda w: pl.BlockSpec((1, w), lambda b, i: (0, 0))
    head_blk = pl.BlockSpec((None, FOX_HEADS, tp, LANES), lambda b, i: (b, 0, i, 0))
    return pl.pallas_call(
        functools.partial(_foxprep_body, tp=tp),
        grid=(B, S // tp),
        in_specs=[blk(BRANCH_W, Z_FQ // BRANCH_W), blk(BRANCH_W, Z_FK // BRANCH_W),
                  blk(BRANCH_W, Z_FV // BRANCH_W), blk(LANES, Z_SM // LANES),
                  vec(LANES), vec(BRANCH_W), vec(BRANCH_W)],
        out_specs=[head_blk, head_blk,
                   pl.BlockSpec((None, FOX_HEADS, V_ROWS, tp), lambda b, i: (b, 0, 0, i))],
        out_shape=[jax.ShapeDtypeStruct((B, FOX_HEADS, S, LANES), MXU_DT)] * 2
        + [jax.ShapeDtypeStruct((B, FOX_HEADS, V_ROWS, S), MXU_DT)],
        scratch_shapes=[pltpu.VMEM((8, LANES), f32)],
        compiler_params=_cparams(("parallel", "arbitrary")),
        name="foxprep",
    )(z3, z3, z3, z3, fb_pad, qg, kg)


def _fox_body(q_ref, k_ref, v_ref, o_ref, m_ref, acc_ref, *, tq):
    i = pl.program_id(2)
    m_ref[...] = jnp.full_like(m_ref, NEG_BIG)
    acc_ref[...] = jnp.zeros_like(acc_ref)

    heads = range(FOX_GROUP)

    def blocks(j, masked):
        start = pl.multiple_of(j * tq, tq)
        s = [lax.dot_general(k_ref[g, pl.ds(start, tq), :], q_ref[g], (((1,), (1,)), ((), ())),
                             preferred_element_type=f32) for g in heads]
        if masked:
            key = lax.broadcasted_iota(jnp.int32, (tq, tq), 0)
            qry = lax.broadcasted_iota(jnp.int32, (tq, tq), 1)
            s = [jnp.where(key <= qry, sg, NEG_BIG) for sg in s]
        m_prev = [m_ref[g, 0:1, :] for g in heads]
        m_new = [jnp.maximum(m_prev[g], jnp.max(s[g], axis=0, keepdims=True)) for g in heads]
        p = [jnp.exp2(s[g] - m_new[g]).astype(q_ref.dtype) for g in heads]
        pv = [jnp.dot(v_ref[g, :, pl.ds(start, tq)], p[g], preferred_element_type=f32) for g in heads]
        for g in heads:
            m_ref[g, 0:1, :] = m_new[g]
            acc_ref[g] = jnp.exp2(m_prev[g] - m_new[g]) * acc_ref[g] + pv[g]

    def full_blocks(jj, carry):
        for u in range(FOX_UNROLL):
            blocks(FOX_UNROLL * jj + u, False)
        return carry

    lax.fori_loop(0, i // FOX_UNROLL, full_blocks, 0)
    for u in range(FOX_UNROLL - 1):
        @pl.when(i % FOX_UNROLL > u)
        def _():
            blocks(i - i % FOX_UNROLL + u, False)

    blocks(i, True)
    for pair in range(FOX_GROUP // 2):
        o_t = jnp.concatenate(
            [acc_ref[g, 0:FOX_HD, :] / acc_ref[g, L_ONE:L_ONE + 1, :] for g in (2 * pair, 2 * pair + 1)], axis=0)
        o_ref[:, pair * LANES:(pair + 1) * LANES] = o_t.T.astype(o_ref.dtype)


FOX_GROUP = 8
FOX_UNROLL = 4


def _fox_attention(qt, kt, vtt):
    B, H, S, _ = qt.shape
    tq = min(256, S)
    G = FOX_GROUP
    assert G == H
    resident = dict(pipeline_mode=pl.Buffered(1))
    return pl.pallas_call(
        functools.partial(_fox_body, tq=tq),
        grid=(B, H // G, S // tq),
        in_specs=[pl.BlockSpec((None, G, tq, LANES), lambda b, h, i: (b, h, i, 0)),
                  pl.BlockSpec((None, G, S, LANES), lambda b, h, i: (b, h, 0, 0), **resident),
                  pl.BlockSpec((None, G, V_ROWS, S), lambda b, h, i: (b, h, 0, 0), **resident)],
        out_specs=pl.BlockSpec((None, tq, H * FOX_HD), lambda b, h, i: (b, i, 0)),
        out_shape=jax.ShapeDtypeStruct((B, S, H * FOX_HD), MXU_DT),
        scratch_shapes=[pltpu.VMEM((G, 8, tq), f32), pltpu.VMEM((G, V_ROWS, tq), f32)],
        compiler_params=_cparams(("parallel", "parallel", "arbitrary")),
        name="fox_attn",
    )(qt, kt, vtt)


CONV_HALO = 32
CONV_ROWS = 64


def _convmix_body(a_ref, g_ref, dw_ref, db_ref, lng_ref, lnb_ref, o_ref, u_ref, us_ref, *, tp):
    @pl.when(pl.program_id(1) == 0)
    def _():
        u_ref[0:CONV_HALO, :] = jnp.zeros((CONV_HALO, BRANCH_W), f32)

    u_ref[CONV_HALO:CONV_HALO + tp, :] = a_ref[...] * jax.nn.sigmoid(g_ref[...])
    n_shift = CONV_HALO + tp - 8
    for rho in range(1, 8):
        us_ref[rho - 1, 0:n_shift, :] = u_ref[rho:rho + n_shift, :]
    for r in range(0, tp, CONV_ROWS):
        acc = jnp.broadcast_to(db_ref[...], (CONV_ROWS, BRANCH_W))
        for k in range(CONV_K):
            q8, rho = divmod(CONV_HALO - (CONV_K - 1) + k, 8)
            off = r + 8 * q8
            tap = u_ref[off:off + CONV_ROWS, :] if rho == 0 else us_ref[rho - 1, off:off + CONV_ROWS, :]
            acc = acc + dw_ref[k:k + 1, :] * tap
        mu = jnp.mean(acc, axis=-1, keepdims=True)
        d = acc - mu
        var = jnp.mean(d * d, axis=-1, keepdims=True)
        y = d * lax.rsqrt(var + EPS) * lng_ref[...] + lnb_ref[...]
        o_ref[r:r + CONV_ROWS, :] = (y * jax.nn.sigmoid(y)).astype(o_ref.dtype)
    u_ref[0:CONV_HALO, :] = u_ref[tp:tp + CONV_HALO, :]


def _convmix(z3, dw_pad, db, lng, lnb):
    B, S, _ = z3.shape
    tp = min(256, S)
    blk = lambda c: pl.BlockSpec((None, tp, BRANCH_W), lambda b, i, c=c: (b, i, c))
    vec = pl.BlockSpec((1, BRANCH_W), lambda b, i: (0, 0))
    return pl.pallas_call(
        functools.partial(_convmix_body, tp=tp),
        grid=(B, S // tp),
        in_specs=[blk(Z_CA // BRANCH_W), blk(Z_CG // BRANCH_W),
                  pl.BlockSpec(dw_pad.shape, lambda b, i: (0, 0)), vec, vec, vec],
        out_specs=blk(0),
        out_shape=jax.ShapeDtypeStruct((B, S, BRANCH_W), MXU_DT),
        scratch_shapes=[pltpu.VMEM((CONV_HALO + tp, BRANCH_W), f32),
                        pltpu.VMEM((7, CONV_HALO + tp, BRANCH_W), f32)],
        compiler_params=_cparams(("parallel", "arbitrary")),
        name="convmix",
    )(z3, z3, dw_pad, db, lng, lnb)


POOL_HALO = 16


def _poolmix_body(u_in_ref, pw_ref, sc_ref, o_ref, u_ref, *, tp):
    i = pl.program_id(1)

    @pl.when(i == 0)
    def _():
        u_ref[0:POOL_HALO, :] = jnp.zeros((POOL_HALO, BRANCH_W), f32)

    u_ref[POOL_HALO:POOL_HALO + tp, :] = u_in_ref[...]
    pos = i * tp + lax.broadcasted_iota(jnp.int32, (tp, 1), 0)
    for gi, w in enumerate(POOL_WINDOWS):
        sl = slice(gi * LANES, (gi + 1) * LANES)
        acc = u_ref[POOL_HALO:POOL_HALO + tp, sl]
        for j in range(1, w):
            acc = acc + u_ref[POOL_HALO - j:POOL_HALO - j + tp, sl]
        cnt = jnp.minimum(pos + 1, w).astype(f32)
        mixed = acc / cnt - u_ref[POOL_HALO:POOL_HALO + tp, sl]
        out = jnp.dot(mixed.astype(pw_ref.dtype), pw_ref[gi], preferred_element_type=f32)
        o_ref[:, sl] = (out * sc_ref[:, sl]).astype(o_ref.dtype)
    u_ref[0:POOL_HALO, :] = u_ref[tp:tp + POOL_HALO, :]


def _poolmix(z3, pw, scale):
    B, S, _ = z3.shape
    tp = min(512, S)
    return pl.pallas_call(
        functools.partial(_poolmix_body, tp=tp),
        grid=(B, S // tp),
        in_specs=[pl.BlockSpec((None, tp, BRANCH_W), lambda b, i: (b, i, Z_PZ // BRANCH_W)),
                  pl.BlockSpec(pw.shape, lambda b, i: (0, 0, 0)),
                  pl.BlockSpec((1, BRANCH_W), lambda b, i: (0, 0))],
        out_specs=pl.BlockSpec((None, tp, BRANCH_W), lambda b, i: (b, i, 0)),
        out_shape=jax.ShapeDtypeStruct((B, S, BRANCH_W), MXU_DT),
        scratch_shapes=[pltpu.VMEM((POOL_HALO + tp, BRANCH_W), f32)],
        compiler_params=_cparams(("parallel", "arbitrary")),
        name="poolmix",
    )(z3, pw, scale)


def _gla_body(q_ref, k_ref, v_ref, r_ref, sm_ref, wa_ref, ba_ref, og_ref, sel_ref, o_ref,
              st_ref, kb_ref, bb_ref, qlo_ref, qhi_ref, kd_ref, vh_ref, dec_ref, oi_ref, up_ref, sb_ref,
              *, tg):
    C = GLA_CHUNK
    hk = GLA_DK // GLA_HEADS
    hv = GLA_DV // GLA_HEADS

    @pl.when(pl.program_id(1) == 0)
    def _():
        st_ref[...] = jnp.zeros_like(st_ref)

    pre = jnp.dot(sm_ref[...].astype(wa_ref.dtype), wa_ref[...], preferred_element_type=f32) + ba_ref[...]
    loga = _log_sigmoid(pre) * (LOG2E / GLA_TEMP)
    row = lax.broadcasted_iota(jnp.int32, (tg, tg), 0)
    col = lax.broadcasted_iota(jnp.int32, (tg, tg), 1)
    same = _group(row, C) == _group(col, C)
    bc = jnp.dot((same & (col <= row)).astype(f32), loga, preferred_element_type=f32, precision=HI)
    blast = jnp.dot(same.astype(f32), loga, preferred_element_type=f32, precision=HI)

    q = q_ref[...] * (hk ** -0.5)
    k = k_ref[...]
    qd = q * jnp.exp2(bc)
    first = (lax.broadcasted_iota(jnp.int32, (1, GLA_DK), 1) & (LANES - 1)) < hk
    qlo_ref[...] = jnp.where(first, qd, 0.0).astype(qlo_ref.dtype)
    qhi_ref[...] = jnp.where(first, 0.0, qd).astype(qhi_ref.dtype)
    kd_ref[...] = (k * jnp.exp2(blast - bc)).astype(kd_ref.dtype)
    dec_ref[...] = jnp.exp2(blast)
    vh_ref[...] = v_ref[...].astype(vh_ref.dtype)

    kb_ref[0:C, :] = jnp.zeros((C, GLA_DK), f32)
    bb_ref[0:C, :] = jnp.zeros((C, GLA_DK), f32)
    kb_ref[C:C + tg, :] = k
    bb_ref[C:C + tg, :] = bc
    rpos = lax.broadcasted_iota(jnp.int32, (tg, 1), 0) & (C - 1)
    lo = lax.broadcasted_iota(jnp.int32, (1, LANES), 1) < hk
    st = [st_ref[p] for p in range(GLA_HEADS // 2)]

    def intra(delta, acc):
        valid = rpos >= delta
        ks = kb_ref[C - delta:C - delta + tg, :]
        bs = bb_ref[C - delta:C - delta + tg, :]
        w = jnp.where(valid, q * ks * jnp.exp2(jnp.where(valid, bc - bs, 0.0)), 0.0)
        return acc + jnp.dot(w.astype(MXU_DT), sel_ref[delta], preferred_element_type=f32)

    n_steps = tg // C
    pairs = range(GLA_HEADS // 2)

    def increment(n):
        rows = slice(n * C, (n + 1) * C)
        for p in pairs:
            ps = slice(p * LANES, (p + 1) * LANES)
            upd = [lax.dot_general(vh_ref[rows, (2 * p + e) * hv:(2 * p + e + 1) * hv], kd_ref[rows, ps],
                                   (((0,), (0,)), ((), ())), preferred_element_type=f32) for e in (0, 1)]
            up_ref[n, p] = jnp.where(lo, upd[0], upd[1])

    def read_out(n):
        rows = slice(n * C, (n + 1) * C)
        for p in pairs:
            ps = slice(p * LANES, (p + 1) * LANES)
            lhs = jnp.concatenate([qlo_ref[rows, ps], qhi_ref[rows, ps]], axis=0)
            o_p = lax.dot_general(lhs, sb_ref[n, p], (((1,), (1,)), ((), ())),
                                  preferred_element_type=f32)
            for e in (0, 1):
                oi_ref[rows, (2 * p + e) * hv:(2 * p + e + 1) * hv] = o_p[e * C:(e + 1) * C]

    per_it = 2 * n_steps // C
    attn = jnp.zeros((tg, GLA_DK), f32)
    for it in range(C // 2):
        attn = intra(it, attn)
        for n in range(it * per_it, min((it + 1) * per_it, n_steps)):
            increment(n)
    for n in range(n_steps):
        for p in pairs:
            sb_ref[n, p] = st[p].astype(sb_ref.dtype)
            st[p] = st[p] * dec_ref[n * C:n * C + 1, p * LANES:(p + 1) * LANES] + up_ref[n, p]
    for p in pairs:
        st_ref[p] = st[p]
    for it in range(C // 2, C):
        attn = intra(it, attn)
        for n in range((it - C // 2) * per_it, min((it - C // 2 + 1) * per_it, n_steps)):
            read_out(n)

    assert tg == GLA_DK
    lane_t = lax.broadcasted_iota(jnp.int32, (1, tg), 1)
    for h in range(GLA_HEADS):
        mine = attn if h == 0 else pltpu.roll(attn, tg - h * hk, axis=1)
        band = pltpu.roll(jnp.where(lane_t < C, mine, 0.0), tg - (C - 1), axis=1, stride=1, stride_axis=0)
        vs = slice(h * hv, (h + 1) * hv)
        oi_ref[:, vs] += jnp.dot(band.astype(MXU_DT), vh_ref[:, vs], preferred_element_type=f32)

    for h in range(GLA_HEADS):
        sl = slice(h * hv, (h + 1) * hv)
        oh = oi_ref[:, sl]
        oh = oh * lax.rsqrt(jnp.mean(oh * oh, axis=-1, keepdims=True) + EPS) * og_ref[:, sl]
        rr = r_ref[:, sl]
        o_ref[:, sl] = (oh * (rr * jax.nn.sigmoid(rr))).astype(o_ref.dtype)


def _gla_selectors():
    hk = GLA_DK // GLA_HEADS
    lane = np.arange(GLA_DK)
    sel = np.zeros((GLA_CHUNK, GLA_DK, GLA_DK), np.float32)
    for delta in range(GLA_CHUNK):
        sel[delta, lane, (lane // hk) * hk + (GLA_CHUNK - 1 - delta)] = 1.0
    return jnp.asarray(sel, MXU_DT)


def _glamix(z3, wa_pad, ba, og):
    B, S, _ = z3.shape
    tg = min(256, S)
    blk = lambda w, c: pl.BlockSpec((None, tg, w), lambda b, i, c=c: (b, i, c))
    vec = lambda w: pl.BlockSpec((1, w), lambda b, i: (0, 0))
    return pl.pallas_call(
        functools.partial(_gla_body, tg=tg),
        grid=(B, S // tg),
        in_specs=[blk(GLA_DK, Z_GQ // GLA_DK), blk(GLA_DK, Z_GK // GLA_DK),
                  blk(GLA_DV, Z_GV // GLA_DV), blk(GLA_DV, Z_GR // GLA_DV),
                  blk(LANES, Z_SM // LANES),
                  pl.BlockSpec((LANES, GLA_DK), lambda b, i: (0, 0)), vec(GLA_DK), vec(GLA_DV),
                  pl.BlockSpec((GLA_CHUNK, GLA_DK, GLA_DK), lambda b, i: (0, 0, 0))],
        out_specs=blk(GLA_DV, 0),
        out_shape=jax.ShapeDtypeStruct((B, S, GLA_DV), MXU_DT),
        scratch_shapes=[pltpu.VMEM((GLA_HEADS // 2, GLA_DV // GLA_HEADS, LANES), f32),
                        pltpu.VMEM((GLA_CHUNK + tg, GLA_DK), f32),
                        pltpu.VMEM((GLA_CHUNK + tg, GLA_DK), f32),
                        pltpu.VMEM((tg, GLA_DK), MXU_DT),
                        pltpu.VMEM((tg, GLA_DK), MXU_DT),
                        pltpu.VMEM((tg, GLA_DK), MXU_DT),
                        pltpu.VMEM((tg, GLA_DV), MXU_DT),
                        pltpu.VMEM((tg, GLA_DK), f32),
                        pltpu.VMEM((tg, GLA_DV), f32),
                        pltpu.VMEM((tg // GLA_CHUNK, GLA_HEADS // 2, GLA_DV // GLA_HEADS, LANES), f32),
                        pltpu.VMEM((tg // GLA_CHUNK, GLA_HEADS // 2, GLA_DV // GLA_HEADS, LANES), MXU_DT)],
        compiler_params=_cparams(("parallel", "arbitrary")),
        name="glamix",
    )(z3, z3, z3, z3, z3, wa_pad, ba, og, _gla_selectors())


def _merge_body(h_ref, oa_ref, ob_ref, oc_ref, od_ref, wg0, wg1, wg2, wg3, gb0, gb1, gb2, gb3,
                p0, p1, p2, p3, y_ref):
    h = h_ref[...]
    acc = None
    for o_ref, wg, gb, p in ((oa_ref, wg0, gb0, p0), (ob_ref, wg1, gb1, p1),
                             (oc_ref, wg2, gb2, p2), (od_ref, wg3, gb3, p3)):
        gate = jax.nn.sigmoid(jnp.dot(h, wg[...], preferred_element_type=f32) + gb[...])
        term = gate * jnp.dot(o_ref[...], p[...], preferred_element_type=f32)
        acc = term if acc is None else acc + term
    y_ref[...] = acc.astype(y_ref.dtype)


def _merge(h, outs, w_gate, gate_b, w_branch, l):
    T, D = h.shape
    tm, tn = min(1024, T), 512
    nj = D // tn
    g0 = Z_COLS // tn
    tok = lambda w: pl.BlockSpec((tm, w), lambda i, j: (i, 0))
    wg = [pl.BlockSpec((None, D, tn), lambda i, j, b=b: (l, 0, g0 + b * nj + j)) for b in range(4)]
    gb = [pl.BlockSpec((1, tn), lambda i, j, b=b: (0, b * nj + j)) for b in range(4)]
    pb = [pl.BlockSpec((None, None, BRANCH_W, tn), lambda i, j, b=b: (l, b, 0, j)) for b in range(4)]
    return pl.pallas_call(
        _merge_body,
        grid=(T // tm, nj),
        in_specs=[tok(D)] + [tok(BRANCH_W)] * 4 + wg + gb + pb,
        out_specs=pl.BlockSpec((tm, tn), lambda i, j: (i, j)),
        out_shape=jax.ShapeDtypeStruct((T, D), MXU_DT),
        compiler_params=_cparams(("parallel", "arbitrary")),
        name="merge",
    )(h, *outs, *([w_gate] * 4), *([gate_b] * 4), *([w_branch] * 4))


def _outproj_body(y_ref, x_ref, w_ref, g_ref, x1_ref, h2_ref):
    x1 = x_ref[...] + jnp.dot(y_ref[...], w_ref[...], preferred_element_type=f32)
    x1_ref[...] = x1
    ms = jnp.mean(x1 * x1, axis=-1, keepdims=True)
    h2_ref[...] = (x1 * lax.rsqrt(ms + EPS) * g_ref[...]).astype(h2_ref.dtype)


def _outproj(y, x2, w_out, g2, l):
    T, D = x2.shape
    tm = min(512, T)
    tok = pl.BlockSpec((tm, D), lambda i: (i, 0))
    return pl.pallas_call(
        _outproj_body,
        grid=(T // tm,),
        in_specs=[tok, tok, pl.BlockSpec((None, D, D), lambda i: (l, 0, 0)),
                  pl.BlockSpec((1, D), lambda i: (0, 0))],
        out_specs=[tok, tok],
        out_shape=[jax.ShapeDtypeStruct((T, D), f32), jax.ShapeDtypeStruct((T, D), MXU_DT)],
        compiler_params=_cparams(("parallel",)),
        name="outproj",
    )(y, x2, w_out, g2)


FFN_HALO = 8


def _ffnup_body(h_ref, wa_ref, wv_ref, dwa_ref, dwv_ref, dba_ref, dbv_ref, g_ref, ua_ref, uv_ref,
                *, tm, tiles_per_seq):
    @pl.when(pl.program_id(1) % tiles_per_seq == 0)
    def _():
        ua_ref[0:FFN_HALO, :] = jnp.zeros((FFN_HALO, ua_ref.shape[1]), f32)
        uv_ref[0:FFN_HALO, :] = jnp.zeros((FFN_HALO, uv_ref.shape[1]), f32)

    h = h_ref[...]

    def conv(w_ref, dw_ref, db_ref, u_ref):
        u_ref[FFN_HALO:FFN_HALO + tm, :] = jnp.dot(h, w_ref[...], preferred_element_type=f32)
        y = db_ref[...] + dw_ref[FFN_K - 1:FFN_K, :] * u_ref[FFN_HALO:FFN_HALO + tm, :]
        for k in range(FFN_K - 1):
            off = FFN_HALO - (FFN_K - 1) + k
            y = y + dw_ref[k:k + 1, :] * u_ref[off:off + tm, :]
        u_ref[0:FFN_HALO, :] = u_ref[tm:tm + FFN_HALO, :]
        return y

    a = conv(wa_ref, dwa_ref, dba_ref, ua_ref)
    v = conv(wv_ref, dwv_ref, dbv_ref, uv_ref)
    g_ref[...] = (a * jax.nn.sigmoid(a) * v).astype(g_ref.dtype)


def _ffnup(h2, ffn_up, ffn_dw, ffn_db, seq_len, l):
    T, D = h2.shape
    dff = ffn_up.shape[2] // 2
    tm, tn = min(1024, seq_len), 512
    nj = dff // tn
    return pl.pallas_call(
        functools.partial(_ffnup_body, tm=tm, tiles_per_seq=seq_len // tm),
        grid=(nj, T // tm),
        in_specs=[pl.BlockSpec((tm, D), lambda j, i: (i, 0)),
                  pl.BlockSpec((None, D, tn), lambda j, i: (l, 0, j)),
                  pl.BlockSpec((None, D, tn), lambda j, i: (l, 0, nj + j)),
                  pl.BlockSpec((FFN_K, tn), lambda j, i: (0, j)),
                  pl.BlockSpec((FFN_K, tn), lambda j, i: (0, nj + j)),
                  pl.BlockSpec((1, tn), lambda j, i: (0, j)),
                  pl.BlockSpec((1, tn), lambda j, i: (0, nj + j))],
        out_specs=pl.BlockSpec((tm, tn), lambda j, i: (i, j)),
        out_shape=jax.ShapeDtypeStruct((T, dff), MXU_DT),
        scratch_shapes=[pltpu.VMEM((FFN_HALO + tm, tn), f32), pltpu.VMEM((FFN_HALO + tm, tn), f32)],
        compiler_params=_cparams(("parallel", "arbitrary")),
        name="ffnup",
    )(h2, ffn_up, ffn_up, ffn_dw, ffn_dw, ffn_db, ffn_db)


def _ffndown_body(g_ref, w_ref, x_ref, o_ref):
    o_ref[...] = x_ref[...] + jnp.dot(g_ref[...], w_ref[...], preferred_element_type=f32)


def _ffndown(g, ffn_down, x1, l):
    T, D = x1.shape
    dff = g.shape[1]
    tm = min(512, T)
    return pl.pallas_call(
        _ffndown_body,
        grid=(T // tm,),
        in_specs=[pl.BlockSpec((tm, dff), lambda i: (i, 0)),
                  pl.BlockSpec((None, dff, D), lambda i: (l, 0, 0), pipeline_mode=pl.Buffered(1)),
                  pl.BlockSpec((tm, D), lambda i: (i, 0))],
        out_specs=pl.BlockSpec((tm, D), lambda i: (i, 0)),
        out_shape=jax.ShapeDtypeStruct((T, D), f32),
        compiler_params=_cparams(("parallel",)),
        name="ffndown",
    )(g, ffn_down, x1)


def _mixers(z3, p):
    B, S, _ = z3.shape
    qt, kt, vt = _foxprep(z3, p["fox_fb"], p["fox_qg"], p["fox_kg"])
    o_a = _fox_attention(qt, kt, vt)
    o_b = _convmix(z3, p["conv_dw"], p["conv_db"], p["conv_ln_g"], p["conv_ln_b"])
    o_c = _glamix(z3, p["gla_wa"], p["gla_ba"], p["gla_og"])
    o_d = _poolmix(z3, p["pool_w"], p["pool_scale"])
    return o_a, o_b, o_c, o_d


STACKED = ("w_cat", "w_branch", "w_out", "ffn_up", "ffn_down")


def _layer(x, params, l):
    B, S, D = x.shape
    T = B * S
    x2 = x.reshape(T, D)
    p = {k: (v if k in STACKED else v[l]) for k, v in params.items()}
    h, z = _inproj(x2, p["norm1_g"], p["w_cat"], l)
    outs = _mixers(z.reshape(B, S, Z_USED), p)
    y = _merge(h, [o.reshape(T, BRANCH_W) for o in outs], p["w_cat"], p["gate_b"], p["w_branch"], l)
    x1, h2 = _outproj(y, x2, p["w_out"], p["norm2_g"], l)
    g = _ffnup(h2, p["ffn_up"], p["ffn_dw"], p["ffn_db"], S, l)
    return _ffndown(g, p["ffn_down"], x1, l).reshape(B, S, D)


def _w_in_pieces(d_model):
    n_small = 3 * BRANCH_W + FOX_HEADS + 2 * BRANCH_W + 2 * GLA_DK + GLA_DV + GLA_RANK + GLA_DV + BRANCH_W
    o_ff = 3 * BRANCH_W
    o_cz = o_ff + FOX_HEADS
    o_ga = o_cz + 2 * BRANCH_W + 2 * GLA_DK + GLA_DV
    o_gr = o_ga + GLA_RANK
    return n_small, [(0, o_ff, Z_FQ), (o_cz, o_ga, Z_CA), (o_gr, n_small, Z_GR),
                     (o_ff, o_cz, Z_SM + SM_FF), (o_ga, o_gr, Z_SM + SM_GA),
                     (n_small, n_small + 4 * d_model, Z_COLS)]


RELAYOUT_CHUNK = 512


def _relayout_body(w_ref, o_ref, *, pieces):
    td = o_ref.shape[0]
    narrow = []
    for src0, src1, dst in pieces:
        if (src1 - src0) % RELAYOUT_CHUNK:
            narrow.append((src0, src1, dst))
            continue
        for c in range(0, src1 - src0, RELAYOUT_CHUNK):
            o_ref[:, dst + c:dst + c + RELAYOUT_CHUNK] = (
                w_ref[src0 + c:src0 + c + RELAYOUT_CHUNK, :].T.astype(o_ref.dtype))
    assert narrow[0][2] == Z_SM and all(a[2] + a[1] - a[0] == b[2] for a, b in zip(narrow, narrow[1:]))
    used = sum(s1 - s0 for s0, s1, _ in narrow)
    rows = [w_ref[s0:s1, :] for s0, s1, _ in narrow] + [jnp.zeros((LANES - used, td), f32)]
    o_ref[:, Z_SM:Z_SM + LANES] = jnp.concatenate(rows, axis=0).T.astype(o_ref.dtype)
    o_ref[:, Z_SM + LANES:Z_COLS] = jnp.zeros((td, Z_COLS - Z_SM - LANES), o_ref.dtype)


def _relayout_w_in(w_in):
    L, D, n_in = w_in.shape
    _, pieces = _w_in_pieces(D)
    n_out = Z_COLS + 4 * D
    td = 256
    return pl.pallas_call(
        functools.partial(_relayout_body, pieces=pieces),
        grid=(L, D // td),
        in_specs=[pl.BlockSpec((None, n_in, td), lambda l, i: (l, 0, i))],
        out_specs=pl.BlockSpec((None, td, n_out), lambda l, i: (l, i, 0)),
        out_shape=jax.ShapeDtypeStruct((L, D, n_out), MXU_DT),
        compiler_params=_cparams(("parallel", "parallel")),
        name="relayout_w_in",
    )(jnp.swapaxes(w_in, 1, 2))


def _prepare(norm1_g, w_in, fox_fb, fox_qg, fox_kg, conv_dw, conv_db, conv_ln_g, conv_ln_b,
             gla_wa, gla_ba, gla_og, pool_w, pool_scale, gate_b, w_branch, w_out,
             norm2_g, ffn_up, ffn_dw, ffn_db, ffn_down):
    w_cat = _relayout_w_in(w_in)
    row = lambda a: a[:, None, :]
    pad_lanes = lambda a, off: jnp.pad(a, ((0, 0), (off, LANES - off - a.shape[1])))
    wa_pad = jnp.pad(gla_wa, ((0, 0), (SM_GA, LANES - SM_GA - GLA_RANK), (0, 0)))
    return {
        "norm1_g": row(norm1_g),
        "w_cat": w_cat,
        "fox_fb": row(pad_lanes(fox_fb, SM_FF)),
        "fox_qg": row(jnp.tile(fox_qg, (1, FOX_HEADS))),
        "fox_kg": row(jnp.tile(fox_kg, (1, FOX_HEADS))),
        "conv_dw": jnp.pad(conv_dw, ((0, 0), (0, 1), (0, 0))),
        "conv_db": row(conv_db), "conv_ln_g": row(conv_ln_g), "conv_ln_b": row(conv_ln_b),
        "gla_wa": wa_pad.astype(MXU_DT), "gla_ba": row(gla_ba), "gla_og": row(gla_og),
        "pool_w": pool_w.astype(MXU_DT), "pool_scale": row(pool_scale),
        "gate_b": row(gate_b),
        "w_branch": w_branch.astype(MXU_DT),
        "w_out": w_out.astype(MXU_DT),
        "norm2_g": row(norm2_g),
        "ffn_up": ffn_up.astype(MXU_DT), "ffn_dw": ffn_dw, "ffn_db": row(ffn_db),
        "ffn_down": ffn_down.astype(MXU_DT),
    }


def kernel(x, norm1_g, w_in, fox_fb, fox_qg, fox_kg, conv_dw, conv_db, conv_ln_g, conv_ln_b, gla_wa, gla_ba, gla_og, pool_w, pool_scale, gate_b, w_branch, w_out, norm2_g, ffn_up, ffn_dw, ffn_db, ffn_down):
    params = _prepare(norm1_g, w_in, fox_fb, fox_qg, fox_kg, conv_dw, conv_db, conv_ln_g, conv_ln_b,
                      gla_wa, gla_ba, gla_og, pool_w, pool_scale, gate_b, w_branch, w_out,
                      norm2_g, ffn_up, ffn_dw, ffn_db, ffn_down)
    for l in range(w_in.shape[0]):
        x = _layer(x, params, l)
    return x
```
